```python
import jax
import jax.numpy as jnp
from jax import lax
import numpy as np

D_MODEL = 1024
BATCH = 8
SEQ = 2048
DEPTH = 2
DEC_BATCH = 128
DEC_SEQ = 1
PAST_LEN = 16384
PAGE_SIZE = 128

HEAD_DIM = 64
GROUP_WIDTH = D_MODEL // 4
MIX_WIDTH = 4 * GROUP_WIDTH
N_DELTA_HEADS = GROUP_WIDTH // HEAD_DIM
N_RET_HEADS = GROUP_WIDTH // HEAD_DIM
POOL_WINDOWS = (2, 4, 8, 16)
POOL_GROUP = GROUP_WIDTH // len(POOL_WINDOWS)
POOL_BUF = max(POOL_WINDOWS) - 1
DELTA_CONV = 4
CONF_CONV = 31
CHUNK = 64
ROPE_BASE = 10000.0
EPS = 1e-6
OFF_DQKV = 0
OFF_DZ = OFF_DQKV + 3 * GROUP_WIDTH
OFF_DA = OFF_DZ + GROUP_WIDTH
OFF_DB = OFF_DA + N_DELTA_HEADS
OFF_PU = OFF_DB + N_DELTA_HEADS
OFF_RQ = OFF_PU + GROUP_WIDTH
OFF_RK = OFF_RQ + GROUP_WIDTH
OFF_RV = OFF_RK + GROUP_WIDTH
OFF_RG = OFF_RV + GROUP_WIDTH
OFF_CG = OFF_RG + GROUP_WIDTH
IN_WIDTH = OFF_CG + 2 * GROUP_WIDTH
PEER_HEADS = 8
PEER_NKEYS = 128
PEER_EXPERTS = PEER_NKEYS * PEER_NKEYS
PEER_TOPK = 16
PEER_DKEY = 256
PEER_DHALF = PEER_DKEY // 2
PEER_BLOCK = 256

kernel_name = "hybrid_delta_pool_retention_conformer_peer_step"


def rmsnorm(x, w):
    xf = x.astype(jnp.float32)
    y = xf * lax.rsqrt(jnp.mean(xf * xf, axis=-1, keepdims=True) + EPS)
    return (y * w.astype(jnp.float32)).astype(x.dtype)


def causal_dwconv(u, buf, w):
    full = jnp.concatenate([buf.astype(u.dtype), u], axis=1)
    y = lax.conv_general_dilated(full, w[:, None, :].astype(u.dtype), window_strides=(1,),
                                 padding='VALID', dimension_numbers=('NWC', 'WIO', 'NWC'),
                                 feature_group_count=u.shape[-1])
    return y, full[:, full.shape[1] - (w.shape[0] - 1):]


def _chunk(x, c):
    b, l = x.shape[:2]
    n = -(-l // c)
    x = jnp.pad(x, [(0, 0), (0, n * c - l)] + [(0, 0)] * (x.ndim - 2))
    x = x.reshape((b, n, c) + x.shape[2:])
    return jnp.transpose(x, (1, 0, 3, 2) + tuple(range(4, x.ndim)))


def _unchunk(o, l):
    n, b, h, c, d = o.shape
    return jnp.transpose(o, (1, 0, 3, 2, 4)).reshape(b, n * c, h, d)[:, :l]


def _decay_masks(g, c):
    G = jnp.cumsum(g, axis=-1)
    idx = jnp.arange(c)
    incl = idx[:, None] >= idx[None, :]
    diff = G[..., :, None] - G[..., None, :]
    dmask = jnp.exp(jnp.where(incl, diff, -jnp.inf))
    return G, dmask, idx


def gated_delta_rule(q, k, v, beta, g, S0):
    l = q.shape[1]
    c = min(CHUNK, l)
    q, k, v, beta, g = (_chunk(t, c) for t in (q, k, v, beta, g))
    G, dmask, idx = _decay_masks(g, c)
    strict = idx[:, None] > idx[None, :]
    kb = k * beta[..., None]
    A = jnp.where(strict, jnp.einsum('...id,...jd->...ij', kb, k) * dmask, 0.0)
    eye = jnp.eye(c, dtype=A.dtype)
    T = lax.linalg.triangular_solve(eye + A, jnp.broadcast_to(eye, A.shape), left_side=True, lower=True)
    eG = jnp.exp(G)[..., None]
    U = jnp.einsum('...ij,...jd->...id', T, v * beta[..., None])
    W = jnp.einsum('...ij,...jd->...id', T, kb * eG)
    Qd = q * eG
    Aqk = jnp.einsum('...id,...jd->...ij', q, k) * dmask
    Glast = G[..., -1:]
    Kd = k * jnp.exp(Glast - G)[..., None]
    dlast = jnp.exp(Glast[..., 0])

    def step(S, xs):
        U_i, W_i, Qd_i, A_i, Kd_i, dl_i = xs
        v_new = U_i - jnp.einsum('bhcd,bhde->bhce', W_i, S)
        o = jnp.einsum('bhcd,bhde->bhce', Qd_i, S) + jnp.einsum('bhij,bhje->bhie', A_i, v_new)
        S = S * dl_i[..., None, None] + jnp.einsum('bhcd,bhce->bhde', Kd_i, v_new)
        return S, o

    S, o = lax.scan(step, S0, (U, W, Qd, Aqk, Kd, dlast))
    return _unchunk(o, l), S


def decayed_linear_attn(q, k, v, g, S0):
    l = q.shape[1]
    c = min(CHUNK, l)
    q, k, v, g = (_chunk(t, c) for t in (q, k, v, g))
    G, dmask, _ = _decay_masks(g, c)
    Aqk = jnp.einsum('...id,...jd->...ij', q, k) * dmask
    Qd = q * jnp.exp(G)[..., None]
    Glast = G[..., -1:]
    Kd = k * jnp.exp(Glast - G)[..., None]
    dlast = jnp.exp(Glast[..., 0])

    def step(S, xs):
        A_i, V_i, Qd_i, Kd_i, dl_i = xs
        o = jnp.einsum('bhcd,bhde->bhce', Qd_i, S) + jnp.einsum('bhij,bhje->bhie', A_i, V_i)
        S = S * dl_i[..., None, None] + jnp.einsum('bhcd,bhce->bhde', Kd_i, V_i)
        return S, o

    S, o = lax.scan(step, S0, (Aqk, v, Qd, Kd, dlast))
    return _unchunk(o, l), S


def rotary(x, pos):
    half = x.shape[-1] // 2
    inv = 1.0 / (ROPE_BASE ** (jnp.arange(half, dtype=jnp.float32) / half))
    ang = pos[:, None] * inv[None, :]
    cos = jnp.cos(ang)[None, :, None, :]
    sin = jnp.sin(ang)[None, :, None, :]
    x1, x2 = x[..., :half], x[..., half:]
    return jnp.concatenate([x1 * cos - x2 * sin, x1 * sin + x2 * cos], axis=-1)


def l2norm(x):
    return x * lax.rsqrt(jnp.sum(x * x, axis=-1, keepdims=True) + EPS)


def delta_mixer(z, s_dconv, s_delta, conv_w, a_log, dt_bias, norm_w):
    b, l, _ = z.shape
    H, dh = N_DELTA_HEADS, HEAD_DIM
    f32 = jnp.float32
    qkv, new_buf = causal_dwconv(z[..., OFF_DQKV:OFF_DQKV + 3 * GROUP_WIDTH], s_dconv, conv_w)
    qkv = jax.nn.silu(qkv.astype(f32))
    q = l2norm(qkv[..., :GROUP_WIDTH].reshape(b, l, H, dh)) * (dh ** -0.5)
    k = l2norm(qkv[..., GROUP_WIDTH:2 * GROUP_WIDTH].reshape(b, l, H, dh))
    v = qkv[..., 2 * GROUP_WIDTH:].reshape(b, l, H, dh)
    a = z[..., OFF_DA:OFF_DA + H].astype(f32)
    beta = jax.nn.sigmoid(z[..., OFF_DB:OFF_DB + H].astype(f32))
    g = -jnp.exp(a_log.astype(f32)) * jax.nn.softplus(a + dt_bias.astype(f32))
    o, S = gated_delta_rule(q, k, v, beta, g, s_delta.astype(f32))
    o = o * lax.rsqrt(jnp.mean(o * o, axis=-1, keepdims=True) + EPS) * norm_w.astype(f32)
    o = o.reshape(b, l, GROUP_WIDTH) * jax.nn.silu(z[..., OFF_DZ:OFF_DZ + GROUP_WIDTH].astype(f32))
    return o.astype(z.dtype), new_buf, S


def pool_mixer(z, s_pool, start_pos, pool_w, pool_scale):
    u = z[..., OFF_PU:OFF_PU + GROUP_WIDTH]
    b, l, _ = u.shape
    full = jnp.concatenate([s_pool.astype(u.dtype), u], axis=1)
    new_buf = full[:, l:]
    ff = full.astype(jnp.float32)
    cs = jnp.concatenate([jnp.zeros((b, 1, GROUP_WIDTH), jnp.float32), jnp.cumsum(ff, axis=1)], axis=1)
    t = jnp.arange(l)
    means = []
    for gi, w in enumerate(POOL_WINDOWS):
        sl = slice(gi * POOL_GROUP, (gi + 1) * POOL_GROUP)
        wsum = cs[:, POOL_BUF + 1:POOL_BUF + 1 + l, sl] - cs[:, POOL_BUF + 1 - w:POOL_BUF + 1 - w + l, sl]
        cnt = jnp.minimum(w, start_pos + t + 1).astype(jnp.float32)
        means.append(wsum / cnt[None, :, None])
    p = (jnp.concatenate(means, axis=-1) - ff[:, POOL_BUF:]).reshape(b, l, len(POOL_WINDOWS), POOL_GROUP)
    y = jnp.einsum('blgc,gcd->blgd', p, pool_w.astype(jnp.float32)).reshape(b, l, GROUP_WIDTH)
    y = y * pool_scale.astype(jnp.float32)
    return y.astype(z.dtype), new_buf


def retention_mixer(z, s_ret, start_pos):
    b, l, _ = z.shape
    H, dh = N_RET_HEADS, HEAD_DIM
    f32 = jnp.float32
    q = z[..., OFF_RQ:OFF_RQ + GROUP_WIDTH].astype(f32).reshape(b, l, H, dh)
    k = z[..., OFF_RK:OFF_RK + GROUP_WIDTH].astype(f32).reshape(b, l, H, dh)
    v = z[..., OFF_RV:OFF_RV + GROUP_WIDTH].astype(f32).reshape(b, l, H, dh)
    gate = z[..., OFF_RG:OFF_RG + GROUP_WIDTH].astype(f32)
    pos = jnp.arange(l, dtype=f32) + start_pos
    q = rotary(q, pos)
    k = rotary(k, pos) * (dh ** -0.5)
    log_gamma = jnp.log1p(-jnp.exp2(-5.0 - jnp.arange(H, dtype=f32)))
    g = jnp.broadcast_to(log_gamma, (b, l, H))
    o, S = decayed_linear_attn(q, k, v, g, s_ret.astype(f32))
    mu = jnp.mean(o, axis=-1, keepdims=True)
    var = jnp.mean(jnp.square(o - mu), axis=-1, keepdims=True)
    o = (o - mu) * lax.rsqrt(var + EPS)
    o = jax.nn.silu(gate) * o.reshape(b, l, GROUP_WIDTH)
    return o.astype(z.dtype), S


def conformer_conv_mixer(z, s_conv, dw_w, dw_b, ln_w, ln_b, pw_w):
    f32 = jnp.float32
    a = z[..., OFF_CG:OFF_CG + GROUP_WIDTH]
    gt = z[..., OFF_CG + GROUP_WIDTH:OFF_CG + 2 * GROUP_WIDTH]
    glu = a * jax.nn.sigmoid(gt)
    dc, new_buf = causal_dwconv(glu, s_conv, dw_w)
    dc = dc.astype(f32) + dw_b.astype(f32)
    mu = jnp.mean(dc, axis=-1, keepdims=True)
    var = jnp.mean(jnp.square(dc - mu), axis=-1, keepdims=True)
    hn = (dc - mu) * lax.rsqrt(var + EPS) * ln_w.astype(f32) + ln_b.astype(f32)
    y = jnp.einsum('blc,cd->bld', jax.nn.silu(hn).astype(z.dtype), pw_w)
    return y.astype(z.dtype), new_buf


def peer_ffn(h, wq, keys, U, V):
    b, l, d = h.shape
    n_tok = b * l
    blk = min(PEER_BLOCK, n_tok)
    nb = -(-n_tok // blk)
    hb = jnp.pad(h.reshape(n_tok, d), ((0, nb * blk - n_tok), (0, 0))).reshape(nb, blk, d)
    K = PEER_TOPK

    def one(xb):
        q = jnp.einsum('td,de->te', xb, wq).astype(jnp.float32).reshape(blk, PEER_HEADS, 2, PEER_DHALF)
        s = jnp.einsum('thpc,hpnc->thpn', q, keys.astype(jnp.float32))
        s_top, i_top = lax.top_k(s, K)
        cand = s_top[:, :, 0, :, None] + s_top[:, :, 1, None, :]
        cidx = i_top[:, :, 0, :, None] * PEER_NKEYS + i_top[:, :, 1, None, :]
        best, sel = lax.top_k(cand.reshape(blk, PEER_HEADS, K * K), K)
        eidx = jnp.take_along_axis(cidx.reshape(blk, PEER_HEADS, K * K), sel, axis=-1)
        gate = jax.nn.softmax(best, axis=-1)
        ue = U[eidx]
        ve = V[eidx]
        act = jax.nn.gelu(jnp.einsum('td,thkd->thk', xb, ue).astype(jnp.float32))
        w = (gate * act).astype(xb.dtype)
        return jnp.einsum('thk,thkd->td', w, ve)

    out = lax.map(one, hb).reshape(nb * blk, d)[:n_tok]
    return out.reshape(b, l, d).astype(h.dtype)


def decoder_layer(x, s_delta, s_dconv, s_pool, s_ret, s_conv, p, start_pos):
    h = rmsnorm(x, p['norm1'])
    z = jnp.einsum('bld,de->ble', h, p['w_in'])
    o_a, n_dconv, n_delta = delta_mixer(z, s_dconv, s_delta, p['delta_conv_w'], p['delta_a_log'],
                                        p['delta_dt_bias'], p['delta_norm_w'])
    o_b, n_pool = pool_mixer(z, s_pool, start_pos, p['pool_w'], p['pool_scale'])
    o_c, n_ret = retention_mixer(z, s_ret, start_pos)
    o_d, n_conv = conformer_conv_mixer(z, s_conv, p['conv_dw_w'], p['conv_dw_b'], p['conv_ln_w'],
                                       p['conv_ln_b'], p['conv_pw_w'])
    mix = jnp.concatenate([o_a, o_b, o_c, o_d], axis=-1)
    x = x + jnp.einsum('ble,ed->bld', mix, p['w_out'])
    x = x + peer_ffn(rmsnorm(x, p['norm2']), p['peer_wq'], p['peer_keys'], p['peer_u'], p['peer_v'])
    return x, (n_delta, n_dconv, n_pool, n_ret, n_conv)


def run_trunk(x, s_delta, s_dconv, s_pool, s_ret, s_conv, layer_params, final_norm, start_pos):
    new = ([], [], [], [], [])
    for li in range(DEPTH):
        p_l = {name: arr[li] for name, arr in layer_params.items()}
        x, st = decoder_layer(x, s_delta[li], s_dconv[li], s_pool[li], s_ret[li], s_conv[li], p_l, start_pos)
        for acc, a in zip(new, st):
            acc.append(a.astype(x.dtype))
    return rmsnorm(x, final_norm), tuple(jnp.stack(acc) for acc in new)


def setup_inputs(seed: int = 0) -> dict:
    key = jax.random.key(seed)
    ks = jax.random.split(key, 28)
    f32 = jnp.float32

    def nrm(k, shape, scale):
        return jax.random.normal(k, shape, f32) * scale

    H = N_DELTA_HEADS
    dt = jnp.exp(jax.random.uniform(ks[11], (DEPTH, H), f32, np.log(1e-3), np.log(1e-1)))
    return {
        'x_prompt': nrm(ks[0], (BATCH, SEQ, D_MODEL), 1.0),
        'x_sample': nrm(ks[1], (DEC_BATCH, DEC_SEQ, D_MODEL), 1.0),
        'state_delta': nrm(ks[2], (DEPTH, DEC_BATCH, N_DELTA_HEADS, HEAD_DIM, HEAD_DIM), 0.1),
        'state_delta_conv': nrm(ks[3], (DEPTH, DEC_BATCH, DELTA_CONV - 1, 3 * GROUP_WIDTH), 1.0),
        'state_pool': nrm(ks[4], (DEPTH, DEC_BATCH, POOL_BUF, GROUP_WIDTH), 1.0),
        'state_ret': nrm(ks[5], (DEPTH, DEC_BATCH, N_RET_HEADS, HEAD_DIM, HEAD_DIM), 0.3),
        'state_conv': nrm(ks[6], (DEPTH, DEC_BATCH, CONF_CONV - 1, GROUP_WIDTH), 0.5),
        'norm1': 1.0 + nrm(ks[7], (DEPTH, D_MODEL), 0.02),
        'w_in': nrm(ks[8], (DEPTH, D_MODEL, IN_WIDTH), D_MODEL ** -0.5),
        'delta_conv_w': nrm(ks[9], (DEPTH, DELTA_CONV, 3 * GROUP_WIDTH), DELTA_CONV ** -0.5),
        'delta_a_log': jnp.log(jax.random.uniform(ks[10], (DEPTH, H), f32, 1.0, 16.0)),
        'delta_dt_bias': jnp.log(jnp.expm1(dt)),
        'delta_norm_w': 1.0 + nrm(ks[12], (DEPTH, HEAD_DIM), 0.02),
        'pool_w': nrm(ks[13], (DEPTH, len(POOL_WINDOWS), POOL_GROUP, POOL_GROUP), POOL_GROUP ** -0.5),
        'pool_scale': 1.0 + nrm(ks[14], (DEPTH, GROUP_WIDTH), 0.02),
        'conv_dw_w': nrm(ks[15], (DEPTH, CONF_CONV, GROUP_WIDTH), CONF_CONV ** -0.5),
        'conv_dw_b': nrm(ks[16], (DEPTH, GROUP_WIDTH), 0.02),
        'conv_ln_w': 1.0 + nrm(ks[17], (DEPTH, GROUP_WIDTH), 0.02),
        'conv_ln_b': nrm(ks[18], (DEPTH, GROUP_WIDTH), 0.02),
        'conv_pw_w': nrm(ks[19], (DEPTH, GROUP_WIDTH, GROUP_WIDTH), GROUP_WIDTH ** -0.5),
        'w_out': nrm(ks[20], (DEPTH, MIX_WIDTH, D_MODEL), 0.5 * MIX_WIDTH ** -0.5),
        'norm2': 1.0 + nrm(ks[21], (DEPTH, D_MODEL), 0.02),
        'peer_wq': nrm(ks[22], (DEPTH, D_MODEL, PEER_HEADS * PEER_DKEY), D_MODEL ** -0.5),
        'peer_keys': nrm(ks[23], (DEPTH, PEER_HEADS, 2, PEER_NKEYS, PEER_DHALF), PEER_DHALF ** -0.5),
        'peer_u': nrm(ks[24], (DEPTH, PEER_EXPERTS, D_MODEL), D_MODEL ** -0.5),
        'peer_v': nrm(ks[25], (DEPTH, PEER_EXPERTS, D_MODEL), (PEER_HEADS * PEER_TOPK) ** -0.5),
        'final_norm': 1.0 + nrm(ks[26], (D_MODEL,), 0.02),
    }


def reference(x_prompt, x_sample, state_delta, state_delta_conv, state_pool, state_ret, state_conv,
              norm1, w_in, delta_conv_w, delta_a_log, delta_dt_bias, delta_norm_w, pool_w, pool_scale,
              conv_dw_w, conv_dw_b, conv_ln_w, conv_ln_b, conv_pw_w, w_out, norm2,
              peer_wq, peer_keys, peer_u, peer_v, final_norm):
    layer_params = {
        'norm1': norm1, 'w_in': w_in, 'delta_conv_w': delta_conv_w, 'delta_a_log': delta_a_log,
        'delta_dt_bias': delta_dt_bias, 'delta_norm_w': delta_norm_w, 'pool_w': pool_w,
        'pool_scale': pool_scale, 'conv_dw_w': conv_dw_w, 'conv_dw_b': conv_dw_b,
        'conv_ln_w': conv_ln_w, 'conv_ln_b': conv_ln_b, 'conv_pw_w': conv_pw_w, 'w_out': w_out,
        'norm2': norm2, 'peer_wq': peer_wq, 'peer_keys': peer_keys, 'peer_u': peer_u, 'peer_v': peer_v,
    }
    dt = x_prompt.dtype
    bp = x_prompt.shape[0]
    z_delta = jnp.zeros((DEPTH, bp) + state_delta.shape[2:], dt)
    z_dconv = jnp.zeros((DEPTH, bp) + state_delta_conv.shape[2:], dt)
    z_pool = jnp.zeros((DEPTH, bp) + state_pool.shape[2:], dt)
    z_ret = jnp.zeros((DEPTH, bp) + state_ret.shape[2:], dt)
    z_conv = jnp.zeros((DEPTH, bp) + state_conv.shape[2:], dt)
    y_prompt, (d_p, dc_p, pl_p, r_p, c_p) = run_trunk(
        x_prompt, z_delta, z_dconv, z_pool, z_ret, z_conv, layer_params, final_norm, 0)
    y_sample, (d_s, dc_s, pl_s, r_s, c_s) = run_trunk(
        x_sample, state_delta, state_delta_conv, state_pool, state_ret, state_conv,
        layer_params, final_norm, PAST_LEN)
    return (y_prompt, y_sample, d_p, dc_p, pl_p, r_p, c_p, d_s, dc_s, pl_s, r_s, c_s)
```

```python
import functools

import numpy as np
import jax
import jax.numpy as jnp
from jax import lax
from jax.experimental import pallas as pl
from jax.experimental.pallas import tpu as pltpu

F32 = jnp.float32
BF16 = jnp.bfloat16

D_MODEL = 1024
DEPTH = 2
HEAD_DIM = 64
GROUP_WIDTH = 256
N_HEADS = 4
POOL_WINDOWS = (2, 4, 8, 16)
POOL_GROUP = 64
POOL_BUF = 15
DELTA_CONV = 4
CONF_CONV = 31
CHUNK = 64
ROPE_BASE = 10000.0
EPS = 1e-6
PAST_LEN = 16384

OFF_DQKV = 0
OFF_DZ = 768
OFF_DA = 1024
OFF_DB = 1028
OFF_PU = 1032
OFF_RQ = 1288
OFF_RK = 1544
OFF_RV = 1800
OFF_RG = 2056
OFF_CG = 2312
IN_WIDTH = 2824
Z_AB = 2816
Z_WIDTH = 2944

PEER_HEADS = 8
PEER_NKEYS = 128
PEER_TOPK = 16
PEER_DHALF = 128
PEER_EXPERTS = PEER_NKEYS * PEER_NKEYS

VMEM_LIMIT = 56 * 1024 * 1024

NEG_INF = float("-inf")
NOT_RANKED = 99.0

PEER_CANDS = tuple((k, l) for k in range(1, PEER_TOPK + 1) for l in range(1, PEER_TOPK + 1) if k * l <= PEER_TOPK)


def _cparams(*sem):
    return pltpu.CompilerParams(dimension_semantics=sem, vmem_limit_bytes=VMEM_LIMIT)


def _dot(a, b, ca, cb, precision=None):
    return lax.dot_general(a, b, (((ca,), (cb,)), ((), ())), preferred_element_type=F32, precision=precision)


def _in_proj_kernel(x_ref, nw_ref, w_ref, z_ref):
    x = x_ref[...]
    h = x * lax.rsqrt(jnp.mean(x * x, axis=-1, keepdims=True) + EPS) * nw_ref[...]
    z_ref[...] = _dot(h.astype(BF16), w_ref[...], 1, 0)


def _in_proj(x, nw, w_bf):
    n = x.shape[0]
    tm = min(256, n)
    return pl.pallas_call(
        _in_proj_kernel,
        grid=(n // tm,),
        in_specs=[pl.BlockSpec((tm, D_MODEL), lambda i: (i, 0)),
                  pl.BlockSpec((1, D_MODEL), lambda i: (0, 0)),
                  pl.BlockSpec(w_bf.shape, lambda i: (0, 0))],
        out_specs=pl.BlockSpec((tm, w_bf.shape[1]), lambda i: (i, 0)),
        out_shape=jax.ShapeDtypeStruct((n, w_bf.shape[1]), F32),
        compiler_params=_cparams("arbitrary"),
        name="in_proj",
    )(x, nw, w_bf)


def _out_proj_kernel(x_ref, mix_ref, w_ref, nw_ref, xo_ref, hn_ref):
    y = x_ref[...] + _dot(mix_ref[...].astype(BF16), w_ref[...], 1, 0)
    xo_ref[...] = y
    hn = y * lax.rsqrt(jnp.mean(y * y, axis=-1, keepdims=True) + EPS) * nw_ref[...]
    hn_ref[...] = hn.astype(BF16)


def _out_proj(x, mix, w_bf, nw):
    n = x.shape[0]
    tm = min(256, n)
    tok = pl.BlockSpec((tm, D_MODEL), lambda i: (i, 0))
    return pl.pallas_call(
        _out_proj_kernel,
        grid=(n // tm,),
        in_specs=[tok, tok, pl.BlockSpec(w_bf.shape, lambda i: (0, 0)), pl.BlockSpec((1, D_MODEL), lambda i: (0, 0))],
        out_specs=[tok, tok],
        out_shape=[jax.ShapeDtypeStruct((n, D_MODEL), F32), jax.ShapeDtypeStruct((n, D_MODEL), BF16)],
        compiler_params=_cparams("arbitrary"),
        name="out_proj",
    )(x, mix, w_bf, nw)


def _topk_ranks(s):
    nk, t = s.shape
    iota = lax.broadcasted_iota(jnp.int32, (nk, t), 0).astype(F32)
    kiota = lax.broadcasted_iota(jnp.int32, (PEER_TOPK, t), 0)

    def extract(k, carry):
        s, r, tops = carry
        m = jnp.max(s, axis=0, keepdims=True)
        idx = jnp.min(jnp.where(s == m, iota, float(nk)), axis=0, keepdims=True)
        one = iota == idx
        tops = jnp.where(kiota == k, m, tops)
        return jnp.where(one, NEG_INF, s), jnp.where(one, (k + 1).astype(F32), r), tops

    init = (s, jnp.full((nk, t), NOT_RANKED, F32), jnp.zeros((PEER_TOPK, t), F32))
    _, r, tops = lax.fori_loop(0, PEER_TOPK, extract, init)
    return tops, r


def _select_pairs(a, b):
    cand = [a[k - 1] + b[l - 1] for (k, l) in PEER_CANDS]
    sel = [jnp.zeros_like(cand[0]) for _ in PEER_CANDS]
    zsum = jnp.zeros_like(cand[0])
    m0 = functools.reduce(jnp.maximum, cand)
    for _ in range(PEER_TOPK):
        m = functools.reduce(jnp.maximum, cand)
        zsum = zsum + jnp.exp(m - m0)
        found = jnp.zeros_like(m)
        for ci in range(len(PEER_CANDS)):
            hit = jnp.where(cand[ci] == m, 1.0 - found, 0.0)
            found = found + hit
            sel[ci] = sel[ci] + hit
            cand[ci] = jnp.where(hit > 0.0, NEG_INF, cand[ci])
    height = []
    for k in range(1, PEER_TOPK + 1):
        hk = jnp.zeros_like(zsum)
        for ci, (kk, _) in enumerate(PEER_CANDS):
            if kk == k:
                hk = hk + sel[ci]
        height.append(hk)
    return height, 1.0 / zsum


def _peer_a_kernel(hn_ref, wqt_ref, keys_ref, e0_ref, lrow_ref, r1_ref, e1_ref, q_scr, s_scr, top_scr):
    ta = hn_ref.shape[0]
    nk = PEER_NKEYS
    q_scr[...] = _dot(wqt_ref[...], hn_ref[...], 1, 1).astype(BF16)

    for hp in range(2 * PEER_HEADS):
        h, p = hp // 2, hp % 2
        hp_rows = slice(hp * nk, (hp + 1) * nk)
        s_all = _dot(keys_ref[hp_rows, :], q_scr[hp_rows, :], 1, 0)
        s_scr[hp_rows, :] = s_all
        for lt in range(ta // 128):
            lanes = slice(lt * 128, (lt + 1) * 128)
            tops, r = _topk_ranks(s_all[:, lanes])
            for k in range(PEER_TOPK):
                top_scr[p, k, h:h + 1, lanes] = tops[k:k + 1, :]
            (lrow_ref if p == 0 else r1_ref)[h * nk:(h + 1) * nk, lanes] = r

    a = [top_scr[0, k] for k in range(PEER_TOPK)]
    b = [top_scr[1, k] for k in range(PEER_TOPK)]
    height, inv_z = _select_pairs(a, b)
    for h in range(PEER_HEADS):
        rows = slice(h * nk, (h + 1) * nk)
        r0 = lrow_ref[rows, :]
        lrow = jnp.zeros_like(r0)
        for k in range(1, PEER_TOPK + 1):
            lrow = jnp.where(r0 == float(k), height[k - 1][h:h + 1, :], lrow)
        lrow_ref[rows, :] = lrow
        s0 = s_scr[2 * h * nk:(2 * h + 1) * nk, :]
        s1 = s_scr[(2 * h + 1) * nk:(2 * h + 2) * nk, :]
        e0_ref[rows, :] = jnp.exp(s0 - a[0][h:h + 1, :]) * inv_z[h:h + 1, :]
        e1_ref[rows, :] = jnp.exp(s1 - b[0][h:h + 1, :])


def _peer_a(hn, wqt, keys2, ta):
    n = hn.shape[0]
    rows = PEER_HEADS * PEER_NKEYS
    out = jax.ShapeDtypeStruct((rows, n), F32)
    fac_spec = pl.BlockSpec((rows, ta), lambda t: (0, t))
    return pl.pallas_call(
        _peer_a_kernel,
        grid=(n // ta,),
        in_specs=[
            pl.BlockSpec((ta, D_MODEL), lambda t: (t, 0)),
            pl.BlockSpec(wqt.shape, lambda t: (0, 0)),
            pl.BlockSpec(keys2.shape, lambda t: (0, 0)),
        ],
        out_specs=[fac_spec] * 4,
        out_shape=[out] * 4,
        scratch_shapes=[
            pltpu.VMEM((2 * rows, ta), BF16),
            pltpu.VMEM((2 * rows, ta), F32),
            pltpu.VMEM((2, PEER_TOPK, PEER_HEADS, ta), F32),
        ],
        compiler_params=_cparams("arbitrary"),
        name="peer_topk",
    )(hn, wqt, keys2)


def _gelu_tanh(x):
    return 0.5 * x * (1.0 + jnp.tanh(0.7978845608028654 * (x + 0.044715 * (x * x * x))))


def _peer_b_kernel(hn_ref, x_ref, e0_ref, lrow_ref, r1_ref, e1_ref, u_ref, v_ref, fw_ref, o_ref, acc_ref, w_scr,
                   *, rows_per_chunk, final_norm):
    c = pl.program_id(1)
    nk = PEER_NKEYS
    t = hn_ref.shape[0]

    @pl.when(c == 0)
    def _():
        acc_ref[...] = jnp.zeros_like(acc_ref)

    w_scr[...] = _gelu_tanh(_dot(u_ref[...], hn_ref[...], 1, 1))

    def per_row(ii, carry):
        i = c * rows_per_chunk + ii
        lrow = [jnp.broadcast_to(lrow_ref[pl.ds(h * nk + i, 1), :], (8, t)) for h in range(PEER_HEADS)]
        e0 = [jnp.broadcast_to(e0_ref[pl.ds(h * nk + i, 1), :], (8, t)) for h in range(PEER_HEADS)]
        for jt in range(nk // 8):
            g = jnp.zeros((8, t), F32)
            for h in range(PEER_HEADS):
                rows = slice(h * nk + jt * 8, h * nk + jt * 8 + 8)
                g = g + jnp.where(r1_ref[rows, :] <= lrow[h], e1_ref[rows, :], 0.0) * e0[h]
            wrow = pl.ds(pl.multiple_of(ii * nk + jt * 8, 8), 8)
            w_scr[wrow, :] = w_scr[wrow, :] * g
        return carry

    lax.fori_loop(0, rows_per_chunk, per_row, 0)
    acc_ref[...] += _dot(w_scr[...].astype(BF16), v_ref[...], 0, 0)

    @pl.when(c == pl.num_programs(1) - 1)
    def _():
        y = x_ref[...] + acc_ref[...]
        if final_norm:
            y = y * lax.rsqrt(jnp.mean(y * y, axis=-1, keepdims=True) + EPS) * fw_ref[...]
        o_ref[...] = y


def _peer_b(hn, x, facs, u_bf, v_bf, final_w, t, rows_per_chunk, final_norm):
    n = hn.shape[0]
    rows = PEER_HEADS * PEER_NKEYS
    ec = rows_per_chunk * PEER_NKEYS
    fac_spec = pl.BlockSpec((rows, t), lambda ti, c: (0, ti))
    tok_spec = pl.BlockSpec((t, D_MODEL), lambda ti, c: (ti, 0))
    exp_spec = pl.BlockSpec((ec, D_MODEL), lambda ti, c: (c, 0))
    return pl.pallas_call(
        functools.partial(_peer_b_kernel, rows_per_chunk=rows_per_chunk, final_norm=final_norm),
        grid=(n // t, PEER_EXPERTS // ec),
        in_specs=[tok_spec, tok_spec, fac_spec, fac_spec, fac_spec, fac_spec, exp_spec, exp_spec,
                  pl.BlockSpec((1, D_MODEL), lambda ti, c: (0, 0))],
        out_specs=tok_spec,
        out_shape=jax.ShapeDtypeStruct((n, D_MODEL), F32),
        scratch_shapes=[pltpu.VMEM((t, D_MODEL), F32), pltpu.VMEM((ec, t), F32)],
        compiler_params=_cparams("arbitrary", "arbitrary"),
        name="peer_experts",
    )(hn, x, *facs, u_bf, v_bf, final_w)


def peer_block(hn, x, wqt, keys2, u_bf, v_bf, final_w, final_norm):
    n = hn.shape[0]
    facs = _peer_a(hn, wqt, keys2, min(256, n))
    return _peer_b(hn, x, facs, u_bf, v_bf, final_w, min(512, n), 8, final_norm)


def _causal_dwconv(u, buf, w):
    full = jnp.concatenate([buf.astype(u.dtype), u], axis=1)
    y = lax.conv_general_dilated(full, w[:, None, :].astype(u.dtype), window_strides=(1,), padding='VALID',
                                 dimension_numbers=('NWC', 'WIO', 'NWC'), feature_group_count=u.shape[-1])
    return y, full[:, full.shape[1] - (w.shape[0] - 1):]


def _chunk(x, c):
    b, l = x.shape[:2]
    n = -(-l // c)
    x = jnp.pad(x, [(0, 0), (0, n * c - l)] + [(0, 0)] * (x.ndim - 2))
    x = x.reshape((b, n, c) + x.shape[2:])
    return jnp.transpose(x, (1, 0, 3, 2) + tuple(range(4, x.ndim)))


def _unchunk(o, l):
    n, b, h, c, d = o.shape
    return jnp.transpose(o, (1, 0, 3, 2, 4)).reshape(b, n * c, h, d)[:, :l]


def _decay_masks(g, c):
    G = jnp.cumsum(g, axis=-1)
    idx = jnp.arange(c)
    incl = idx[:, None] >= idx[None, :]
    diff = G[..., :, None] - G[..., None, :]
    dmask = jnp.exp(jnp.where(incl, diff, -jnp.inf))
    return G, dmask, idx


def _gated_delta_rule(q, k, v, beta, g, S0):
    l = q.shape[1]
    c = min(CHUNK, l)
    q, k, v, beta, g = (_chunk(t, c) for t in (q, k, v, beta, g))
    G, dmask, idx = _decay_masks(g, c)
    strict = idx[:, None] > idx[None, :]
    kb = k * beta[..., None]
    A = jnp.where(strict, jnp.einsum('...id,...jd->...ij', kb, k) * dmask, 0.0)
    eye = jnp.eye(c, dtype=A.dtype)
    T = lax.linalg.triangular_solve(eye + A, jnp.broadcast_to(eye, A.shape), left_side=True, lower=True)
    eG = jnp.exp(G)[..., None]
    U = jnp.einsum('...ij,...jd->...id', T, v * beta[..., None])
    W = jnp.einsum('...ij,...jd->...id', T, kb * eG)
    Qd = q * eG
    Aqk = jnp.einsum('...id,...jd->...ij', q, k) * dmask
    Glast = G[..., -1:]
    Kd = k * jnp.exp(Glast - G)[..., None]
    dlast = jnp.exp(Glast[..., 0])

    def step(S, xs):
        U_i, W_i, Qd_i, A_i, Kd_i, dl_i = xs
        v_new = U_i - jnp.einsum('bhcd,bhde->bhce', W_i, S)
        o = jnp.einsum('bhcd,bhde->bhce', Qd_i, S) + jnp.einsum('bhij,bhje->bhie', A_i, v_new)
        S = S * dl_i[..., None, None] + jnp.einsum('bhcd,bhce->bhde', Kd_i, v_new)
        return S, o

    S, o = lax.scan(step, S0, (U, W, Qd, Aqk, Kd, dlast))
    return _unchunk(o, l), S


def _decayed_linear_attn(q, k, v, g, S0):
    l = q.shape[1]
    c = min(CHUNK, l)
    q, k, v, g = (_chunk(t, c) for t in (q, k, v, g))
    G, dmask, _ = _decay_masks(g, c)
    Aqk = jnp.einsum('...id,...jd->...ij', q, k) * dmask
    Qd = q * jnp.exp(G)[..., None]
    Glast = G[..., -1:]
    Kd = k * jnp.exp(Glast - G)[..., None]
    dlast = jnp.exp(Glast[..., 0])

    def step(S, xs):
        A_i, V_i, Qd_i, Kd_i, dl_i = xs
        o = jnp.einsum('bhcd,bhde->bhce', Qd_i, S) + jnp.einsum('bhij,bhje->bhie', A_i, V_i)
        S = S * dl_i[..., None, None] + jnp.einsum('bhcd,bhce->bhde', Kd_i, V_i)
        return S, o

    S, o = lax.scan(step, S0, (Aqk, v, Qd, Kd, dlast))
    return _unchunk(o, l), S


def _rotary(x, pos):
    half = x.shape[-1] // 2
    inv = 1.0 / (ROPE_BASE ** (jnp.arange(half, dtype=F32) / half))
    ang = pos[:, None] * inv[None, :]
    cos = jnp.cos(ang)[None, :, None, :]
    sin = jnp.sin(ang)[None, :, None, :]
    x1, x2 = x[..., :half], x[..., half:]
    return jnp.concatenate([x1 * cos - x2 * sin, x1 * sin + x2 * cos], axis=-1)


def _l2norm(x):
    return x * lax.rsqrt(jnp.sum(x * x, axis=-1, keepdims=True) + EPS)


def _delta_mixer(z, s_dconv, s_delta, conv_w, a_log, dt_bias, norm_w):
    b, l, _ = z.shape
    H, dh = N_HEADS, HEAD_DIM
    qkv, new_buf = _causal_dwconv(z[..., 0:768], s_dconv, conv_w)
    qkv = jax.nn.silu(qkv)
    q = _l2norm(qkv[..., :GROUP_WIDTH].reshape(b, l, H, dh)) * (dh ** -0.5)
    k = _l2norm(qkv[..., GROUP_WIDTH:2 * GROUP_WIDTH].reshape(b, l, H, dh))
    v = qkv[..., 2 * GROUP_WIDTH:].reshape(b, l, H, dh)
    a = z[..., Z_AB:Z_AB + H]
    beta = jax.nn.sigmoid(z[..., Z_AB + H:Z_AB + 2 * H])
    g = -jnp.exp(a_log) * jax.nn.softplus(a + dt_bias)
    o, S = _gated_delta_rule(q, k, v, beta, g, s_delta)
    o = o * lax.rsqrt(jnp.mean(o * o, axis=-1, keepdims=True) + EPS) * norm_w
    o = o.reshape(b, l, GROUP_WIDTH) * jax.nn.silu(z[..., 768:1024])
    return o, new_buf, S


def _pool_mixer(z, s_pool, start_pos, pool_w, pool_scale):
    u = z[..., 1024:1280]
    b, l, _ = u.shape
    full = jnp.concatenate([s_pool, u], axis=1)
    new_buf = full[:, l:]
    cs = jnp.concatenate([jnp.zeros((b, 1, GROUP_WIDTH), F32), jnp.cumsum(full, axis=1)], axis=1)
    t = jnp.arange(l)
    means = []
    for gi, w in enumerate(POOL_WINDOWS):
        sl = slice(gi * POOL_GROUP, (gi + 1) * POOL_GROUP)
        wsum = cs[:, POOL_BUF + 1:POOL_BUF + 1 + l, sl] - cs[:, POOL_BUF + 1 - w:POOL_BUF + 1 - w + l, sl]
        cnt = jnp.minimum(w, start_pos + t + 1).astype(F32)
        means.append(wsum / cnt[None, :, None])
    p = (jnp.concatenate(means, axis=-1) - full[:, POOL_BUF:]).reshape(b, l, len(POOL_WINDOWS), POOL_GROUP)
    y = jnp.einsum('blgc,gcd->blgd', p, pool_w).reshape(b, l, GROUP_WIDTH)
    return y * pool_scale, new_buf


def _retention_mixer(z, s_ret, start_pos):
    b, l, _ = z.shape
    H, dh = N_HEADS, HEAD_DIM
    q = z[..., 1280:1536].reshape(b, l, H, dh)
    k = z[..., 1536:1792].reshape(b, l, H, dh)
    v = z[..., 1792:2048].reshape(b, l, H, dh)
    gate = z[..., 2048:2304]
    pos = jnp.arange(l, dtype=F32) + start_pos
    q = _rotary(q, pos)
    k = _rotary(k, pos) * (dh ** -0.5)
    log_gamma = jnp.log1p(-jnp.exp2(-5.0 - jnp.arange(H, dtype=F32)))
    g = jnp.broadcast_to(log_gamma, (b, l, H))
    o, S = _decayed_linear_attn(q, k, v, g, s_ret)
    mu = jnp.mean(o, axis=-1, keepdims=True)
    var = jnp.mean(jnp.square(o - mu), axis=-1, keepdims=True)
    o = (o - mu) * lax.rsqrt(var + EPS)
    return jax.nn.silu(gate) * o.reshape(b, l, GROUP_WIDTH), S


def _conformer_mixer(z, s_conv, dw_w, dw_b, ln_w, ln_b, pw_w):
    a = z[..., 2304:2560]
    gt = z[..., 2560:2816]
    glu = a * jax.nn.sigmoid(gt)
    dc, new_buf = _causal_dwconv(glu, s_conv, dw_w)
    dc = dc + dw_b
    mu = jnp.mean(dc, axis=-1, keepdims=True)
    var = jnp.mean(jnp.square(dc - mu), axis=-1, keepdims=True)
    hn = (dc - mu) * lax.rsqrt(var + EPS) * ln_w + ln_b
    return jnp.einsum('blc,cd->bld', jax.nn.silu(hn), pw_w), new_buf


def _permute_w_in(w_in):
    pad = jnp.zeros((w_in.shape[0], Z_WIDTH - Z_AB - 8), w_in.dtype)
    return jnp.concatenate([w_in[:, :OFF_DA], w_in[:, OFF_PU:], w_in[:, OFF_DA:OFF_PU], pad], axis=1)


def _run_trunk(x, states, params, final_norm, start_pos):
    b, l, _ = x.shape
    n = b * l
    xf = x.reshape(n, D_MODEL)
    s_delta, s_dconv, s_pool, s_ret, s_conv = states
    new = ([], [], [], [], [])
    for li in range(DEPTH):
        p = {name: arr[li] for name, arr in params.items()}
        z = _in_proj(xf, p['norm1'].reshape(1, D_MODEL), p['w_in_bf']).reshape(b, l, Z_WIDTH)
        o_a, n_dconv, n_delta = _delta_mixer(z, s_dconv[li], s_delta[li], p['delta_conv_w'], p['delta_a_log'],
                                             p['delta_dt_bias'], p['delta_norm_w'])
        o_b, n_pool = _pool_mixer(z, s_pool[li], start_pos, p['pool_w'], p['pool_scale'])
        o_c, n_ret = _retention_mixer(z, s_ret[li], start_pos)
        o_d, n_conv = _conformer_mixer(z, s_conv[li], p['conv_dw_w'], p['conv_dw_b'], p['conv_ln_w'],
                                       p['conv_ln_b'], p['conv_pw_w'])
        mix = jnp.concatenate([o_a, o_b, o_c, o_d], axis=-1).reshape(n, D_MODEL)
        xf, hn = _out_proj(xf, mix, p['w_out_bf'], p['norm2'].reshape(1, D_MODEL))
        xf = peer_block(hn, xf, p['wqt_bf'], p['keys_bf'], p['u_bf'], p['v_bf'], final_norm.reshape(1, D_MODEL),
                        li == DEPTH - 1)
        for acc, a in zip(new, (n_delta, n_dconv, n_pool, n_ret, n_conv)):
            acc.append(a)
    return xf.reshape(b, l, D_MODEL), tuple(jnp.stack(acc) for acc in new)


def kernel(x_prompt, x_sample, state_delta, state_delta_conv, state_pool, state_ret, state_conv, norm1, w_in, delta_conv_w, delta_a_log, delta_dt_bias, delta_norm_w, pool_w, pool_scale, conv_dw_w, conv_dw_b, conv_ln_w, conv_ln_b, conv_pw_w, w_out, norm2, peer_wq, peer_keys, peer_u, peer_v, final_norm):
    params = {
        'norm1': norm1, 'delta_conv_w': delta_conv_w, 'delta_a_log': delta_a_log, 'delta_dt_bias': delta_dt_bias,
        'delta_norm_w': delta_norm_w, 'pool_w': pool_w, 'pool_scale': pool_scale, 'conv_dw_w': conv_dw_w,
        'conv_dw_b': conv_dw_b, 'conv_ln_w': conv_ln_w, 'conv_ln_b': conv_ln_b, 'conv_pw_w': conv_pw_w,
        'norm2': norm2,
        'w_in_bf': jax.vmap(_permute_w_in)(w_in).astype(BF16),
        'w_out_bf': w_out.astype(BF16),
        'wqt_bf': jnp.swapaxes(peer_wq, 1, 2).astype(BF16),
        'keys_bf': peer_keys.reshape(DEPTH, 2 * PEER_HEADS * PEER_NKEYS, PEER_DHALF).astype(BF16),
        'u_bf': peer_u.astype(BF16),
        'v_bf': peer_v.astype(BF16),
    }
    bp = x_prompt.shape[0]
    zeros = tuple(jnp.zeros((DEPTH, bp) + s.shape[2:], F32)
                  for s in (state_delta, state_delta_conv, state_pool, state_ret, state_conv))
    y_p, st_p = _run_trunk(x_prompt, zeros, params, final_norm, 0)
    y_s, st_s = _run_trunk(x_sample, (state_delta, state_delta_conv, state_pool, state_ret, state_conv),
                           params, final_norm, PAST_LEN)
    return (y_p, y_s) + st_p + st_s
```

```python
import functools

import numpy as np
import jax
import jax.numpy as jnp
from jax import lax
from jax.experimental import pallas as pl
from jax.experimental.pallas import tpu as pltpu

F32 = jnp.float32
BF16 = jnp.bfloat16

D_MODEL = 1024
DEPTH = 2
HEAD_DIM = 64
GROUP_WIDTH = 256
N_HEADS = 4
POOL_WINDOWS = (2, 4, 8, 16)
POOL_GROUP = 64
POOL_BUF = 15
DELTA_CONV = 4
CONF_CONV = 31
CHUNK = 64
ROPE_BASE = 10000.0
EPS = 1e-6
PAST_LEN = 16384

OFF_DQKV = 0
OFF_DZ = 768
OFF_DA = 1024
OFF_DB = 1028
OFF_PU = 1032
OFF_RQ = 1288
OFF_RK = 1544
OFF_RV = 1800
OFF_RG = 2056
OFF_CG = 2312
IN_WIDTH = 2824
Z_AB = 2816
Z_WIDTH = 2944

PEER_HEADS = 8
PEER_NKEYS = 128
PEER_TOPK = 16
PEER_DHALF = 128
PEER_EXPERTS = PEER_NKEYS * PEER_NKEYS

VMEM_LIMIT = 56 * 1024 * 1024

NEG_INF = float("-inf")
NOT_RANKED = 99.0

PEER_CANDS = tuple((k, l) for k in range(1, PEER_TOPK + 1) for l in range(1, PEER_TOPK + 1) if k * l <= PEER_TOPK)


def _cparams(*sem):
    return pltpu.CompilerParams(dimension_semantics=sem, vmem_limit_bytes=VMEM_LIMIT)


def _dot(a, b, ca, cb, precision=None):
    return lax.dot_general(a, b, (((ca,), (cb,)), ((), ())), preferred_element_type=F32, precision=precision)


def _in_proj_kernel(x_ref, nw_ref, w_ref, z_ref):
    x = x_ref[...]
    h = x * lax.rsqrt(jnp.mean(x * x, axis=-1, keepdims=True) + EPS) * nw_ref[...]
    z_ref[...] = _dot(h.astype(BF16), w_ref[...], 1, 0)


def _in_proj(x, nw, w_bf):
    n = x.shape[0]
    tm = min(256, n)
    return pl.pallas_call(
        _in_proj_kernel,
        grid=(n // tm,),
        in_specs=[pl.BlockSpec((tm, D_MODEL), lambda i: (i, 0)),
                  pl.BlockSpec((1, D_MODEL), lambda i: (0, 0)),
                  pl.BlockSpec(w_bf.shape, lambda i: (0, 0))],
        out_specs=pl.BlockSpec((tm, w_bf.shape[1]), lambda i: (i, 0)),
        out_shape=jax.ShapeDtypeStruct((n, w_bf.shape[1]), F32),
        compiler_params=_cparams("arbitrary"),
        name="in_proj",
    )(x, nw, w_bf)


def _out_proj_kernel(x_ref, mix_ref, w_ref, nw_ref, xo_ref, hn_ref):
    y = x_ref[...] + _dot(mix_ref[...].astype(BF16), w_ref[...], 1, 0)
    xo_ref[...] = y
    hn = y * lax.rsqrt(jnp.mean(y * y, axis=-1, keepdims=True) + EPS) * nw_ref[...]
    hn_ref[...] = hn.astype(BF16)


def _out_proj(x, mix, w_bf, nw):
    n = x.shape[0]
    tm = min(256, n)
    tok = pl.BlockSpec((tm, D_MODEL), lambda i: (i, 0))
    return pl.pallas_call(
        _out_proj_kernel,
        grid=(n // tm,),
        in_specs=[tok, tok, pl.BlockSpec(w_bf.shape, lambda i: (0, 0)), pl.BlockSpec((1, D_MODEL), lambda i: (0, 0))],
        out_specs=[tok, tok],
        out_shape=[jax.ShapeDtypeStruct((n, D_MODEL), F32), jax.ShapeDtypeStruct((n, D_MODEL), BF16)],
        compiler_params=_cparams("arbitrary"),
        name="out_proj",
    )(x, mix, w_bf, nw)


def _topk_ranks(s):
    nk, t = s.shape
    iota = lax.broadcasted_iota(jnp.int32, (nk, t), 0).astype(F32)
    kiota = lax.broadcasted_iota(jnp.int32, (PEER_TOPK, t), 0)

    def extract(k, carry):
        s, r, tops = carry
        m = jnp.max(s, axis=0, keepdims=True)
        idx = jnp.min(jnp.where(s == m, iota, float(nk)), axis=0, keepdims=True)
        one = iota == idx
        tops = jnp.where(kiota == k, m, tops)
        return jnp.where(one, NEG_INF, s), jnp.where(one, (k + 1).astype(F32), r), tops

    init = (s, jnp.full((nk, t), NOT_RANKED, F32), jnp.zeros((PEER_TOPK, t), F32))
    _, r, tops = lax.fori_loop(0, PEER_TOPK, extract, init)
    return tops, r


def _select_pairs(a, b):
    cand = [a[k - 1] + b[l - 1] for (k, l) in PEER_CANDS]
    sel = [jnp.zeros_like(cand[0]) for _ in PEER_CANDS]
    zsum = jnp.zeros_like(cand[0])
    m0 = functools.reduce(jnp.maximum, cand)
    for _ in range(PEER_TOPK):
        m = functools.reduce(jnp.maximum, cand)
        zsum = zsum + jnp.exp(m - m0)
        found = jnp.zeros_like(m)
        for ci in range(len(PEER_CANDS)):
            hit = jnp.where(cand[ci] == m, 1.0 - found, 0.0)
            found = found + hit
            sel[ci] = sel[ci] + hit
            cand[ci] = jnp.where(hit > 0.0, NEG_INF, cand[ci])
    height = []
    for k in range(1, PEER_TOPK + 1):
        hk = jnp.zeros_like(zsum)
        for ci, (kk, _) in enumerate(PEER_CANDS):
            if kk == k:
                hk = hk + sel[ci]
        height.append(hk)
    return height, 1.0 / zsum


def _peer_a_kernel(hn_ref, wqt_ref, keys_ref, e0_ref, lrow_ref, r1_ref, e1_ref, q_scr, s_scr, top_scr):
    ta = hn_ref.shape[0]
    nk = PEER_NKEYS
    q_scr[...] = _dot(wqt_ref[...], hn_ref[...], 1, 1).astype(BF16)

    for hp in range(2 * PEER_HEADS):
        h, p = hp // 2, hp % 2
        hp_rows = slice(hp * nk, (hp + 1) * nk)
        s_all = _dot(keys_ref[hp_rows, :], q_scr[hp_rows, :], 1, 0)
        s_scr[hp_rows, :] = s_all
        for lt in range(ta // 128):
            lanes = slice(lt * 128, (lt + 1) * 128)
            tops, r = _topk_ranks(s_all[:, lanes])
            for k in range(PEER_TOPK):
                top_scr[p, k, h:h + 1, lanes] = tops[k:k + 1, :]
            (lrow_ref if p == 0 else r1_ref)[h * nk:(h + 1) * nk, lanes] = r

    a = [top_scr[0, k] for k in range(PEER_TOPK)]
    b = [top_scr[1, k] for k in range(PEER_TOPK)]
    height, inv_z = _select_pairs(a, b)
    for h in range(PEER_HEADS):
        rows = slice(h * nk, (h + 1) * nk)
        r0 = lrow_ref[rows, :]
        lrow = jnp.zeros_like(r0)
        for k in range(1, PEER_TOPK + 1):
            lrow = jnp.where(r0 == float(k), height[k - 1][h:h + 1, :], lrow)
        lrow_ref[rows, :] = lrow
        s0 = s_scr[2 * h * nk:(2 * h + 1) * nk, :]
        s1 = s_scr[(2 * h + 1) * nk:(2 * h + 2) * nk, :]
        e0_ref[rows, :] = jnp.exp(s0 - a[0][h:h + 1, :]) * inv_z[h:h + 1, :]
        e1_ref[rows, :] = jnp.exp(s1 - b[0][h:h + 1, :])


def _peer_a(hn, wqt, keys2, ta):
    n = hn.shape[0]
    rows = PEER_HEADS * PEER_NKEYS
    out = jax.ShapeDtypeStruct((rows, n), F32)
    fac_spec = pl.BlockSpec((rows, ta), lambda t: (0, t))
    return pl.pallas_call(
        _peer_a_kernel,
        grid=(n // ta,),
        in_specs=[
            pl.BlockSpec((ta, D_MODEL), lambda t: (t, 0)),
            pl.BlockSpec(wqt.shape, lambda t: (0, 0)),
            pl.BlockSpec(keys2.shape, lambda t: (0, 0)),
        ],
        out_specs=[fac_spec] * 4,
        out_shape=[out] * 4,
        scratch_shapes=[
            pltpu.VMEM((2 * rows, ta), BF16),
            pltpu.VMEM((2 * rows, ta), F32),
            pltpu.VMEM((2, PEER_TOPK, PEER_HEADS, ta), F32),
        ],
        compiler_params=_cparams("arbitrary"),
        name="peer_topk",
    )(hn, wqt, keys2)


def _gelu_tanh(x):
    return 0.5 * x * (1.0 + jnp.tanh(0.7978845608028654 * (x + 0.044715 * (x * x * x))))


def _peer_b_kernel(hn_ref, x_ref, e0_ref, lrow_ref, r1_ref, e1_ref, u_ref, v_ref, fw_ref, o_ref, acc_ref, w_scr,
                   *, rows_per_chunk, final_norm):
    c = pl.program_id(1)
    nk = PEER_NKEYS
    t = hn_ref.shape[0]

    @pl.when(c == 0)
    def _():
        acc_ref[...] = jnp.zeros_like(acc_ref)

    w_scr[...] = _gelu_tanh(_dot(u_ref[...], hn_ref[...], 1, 1))

    def per_row(ii, carry):
        i = c * rows_per_chunk + ii
        lrow = [jnp.broadcast_to(lrow_ref[pl.ds(h * nk + i, 1), :], (8, t)) for h in range(PEER_HEADS)]
        e0 = [jnp.broadcast_to(e0_ref[pl.ds(h * nk + i, 1), :], (8, t)) for h in range(PEER_HEADS)]
        for jt in range(nk // 8):
            g = jnp.zeros((8, t), F32)
            for h in range(PEER_HEADS):
                rows = slice(h * nk + jt * 8, h * nk + jt * 8 + 8)
                g = g + jnp.where(r1_ref[rows, :] <= lrow[h], e1_ref[rows, :], 0.0) * e0[h]
            wrow = pl.ds(pl.multiple_of(ii * nk + jt * 8, 8), 8)
            w_scr[wrow, :] = w_scr[wrow, :] * g
        return carry

    lax.fori_loop(0, rows_per_chunk, per_row, 0)
    acc_ref[...] += _dot(w_scr[...].astype(BF16), v_ref[...], 0, 0)

    @pl.when(c == pl.num_programs(1) - 1)
    def _():
        y = x_ref[...] + acc_ref[...]
        if final_norm:
            y = y * lax.rsqrt(jnp.mean(y * y, axis=-1, keepdims=True) + EPS) * fw_ref[...]
        o_ref[...] = y


def _peer_b(hn, x, facs, u_bf, v_bf, final_w, t, rows_per_chunk, final_norm):
    n = hn.shape[0]
    rows = PEER_HEADS * PEER_NKEYS
    ec = rows_per_chunk * PEER_NKEYS
    fac_spec = pl.BlockSpec((rows, t), lambda ti, c: (0, ti))
    tok_spec = pl.BlockSpec((t, D_MODEL), lambda ti, c: (ti, 0))
    exp_spec = pl.BlockSpec((ec, D_MODEL), lambda ti, c: (c, 0))
    return pl.pallas_call(
        functools.partial(_peer_b_kernel, rows_per_chunk=rows_per_chunk, final_norm=final_norm),
        grid=(n // t, PEER_EXPERTS // ec),
        in_specs=[tok_spec, tok_spec, fac_spec, fac_spec, fac_spec, fac_spec, exp_spec, exp_spec,
                  pl.BlockSpec((1, D_MODEL), lambda ti, c: (0, 0))],
        out_specs=tok_spec,
        out_shape=jax.ShapeDtypeStruct((n, D_MODEL), F32),
        scratch_shapes=[pltpu.VMEM((t, D_MODEL), F32), pltpu.VMEM((ec, t), F32)],
        compiler_params=_cparams("arbitrary", "arbitrary"),
        name="peer_experts",
    )(hn, x, *facs, u_bf, v_bf, final_w)


def peer_block(hn, x, wqt, keys2, u_bf, v_bf, final_w, final_norm):
    n = hn.shape[0]
    facs = _peer_a(hn, wqt, keys2, min(256, n))
    return _peer_b(hn, x, facs, u_bf, v_bf, final_w, min(512, n), 8, final_norm)


def _causal_dwconv(u, buf, w):
    full = jnp.concatenate([buf.astype(u.dtype), u], axis=1)
    y = lax.conv_general_dilated(full, w[:, None, :].astype(u.dtype), window_strides=(1,), padding='VALID',
                                 dimension_numbers=('NWC', 'WIO', 'NWC'), feature_group_count=u.shape[-1])
    return y, full[:, full.shape[1] - (w.shape[0] - 1):]


def _chunk(x, c):
    b, l = x.shape[:2]
    n = -(-l // c)
    x = jnp.pad(x, [(0, 0), (0, n * c - l)] + [(0, 0)] * (x.ndim - 2))
    x = x.reshape((b, n, c) + x.shape[2:])
    return jnp.transpose(x, (1, 0, 3, 2) + tuple(range(4, x.ndim)))


def _unchunk(o, l):
    n, b, h, c, d = o.shape
    return jnp.transpose(o, (1, 0, 3, 2, 4)).reshape(b, n * c, h, d)[:, :l]


def _decay_masks(g, c):
    G = jnp.cumsum(g, axis=-1)
    idx = jnp.arange(c)
    incl = idx[:, None] >= idx[None, :]
    diff = G[..., :, None] - G[..., None, :]
    dmask = jnp.exp(jnp.where(incl, diff, -jnp.inf))
    return G, dmask, idx


def _gated_delta_rule(q, k, v, beta, g, S0):
    l = q.shape[1]
    c = min(CHUNK, l)
    q, k, v, beta, g = (_chunk(t, c) for t in (q, k, v, beta, g))
    G, dmask, idx = _decay_masks(g, c)
    strict = idx[:, None] > idx[None, :]
    kb = k * beta[..., None]
    A = jnp.where(strict, jnp.einsum('...id,...jd->...ij', kb, k) * dmask, 0.0)
    eye = jnp.eye(c, dtype=A.dtype)
    T = lax.linalg.triangular_solve(eye + A, jnp.broadcast_to(eye, A.shape), left_side=True, lower=True)
    eG = jnp.exp(G)[..., None]
    U = jnp.einsum('...ij,...jd->...id', T, v * beta[..., None])
    W = jnp.einsum('...ij,...jd->...id', T, kb * eG)
    Qd = q * eG
    Aqk = jnp.einsum('...id,...jd->...ij', q, k) * dmask
    Glast = G[..., -1:]
    Kd = k * jnp.exp(Glast - G)[..., None]
    dlast = jnp.exp(Glast[..., 0])

    def step(S, xs):
        U_i, W_i, Qd_i, A_i, Kd_i, dl_i = xs
        v_new = U_i - jnp.einsum('bhcd,bhde->bhce', W_i, S)
        o = jnp.einsum('bhcd,bhde->bhce', Qd_i, S) + jnp.einsum('bhij,bhje->bhie', A_i, v_new)
        S = S * dl_i[..., None, None] + jnp.einsum('bhcd,bhce->bhde', Kd_i, v_new)
        return S, o

    S, o = lax.scan(step, S0, (U, W, Qd, Aqk, Kd, dlast))
    return _unchunk(o, l), S


def _decayed_linear_attn(q, k, v, g, S0):
    l = q.shape[1]
    c = min(CHUNK, l)
    q, k, v, g = (_chunk(t, c) for t in (q, k, v, g))
    G, dmask, _ = _decay_masks(g, c)
    Aqk = jnp.einsum('...id,...jd->...ij', q, k) * dmask
    Qd = q * jnp.exp(G)[..., None]
    Glast = G[..., -1:]
    Kd = k * jnp.exp(Glast - G)[..., None]
    dlast = jnp.exp(Glast[..., 0])

    def step(S, xs):
        A_i, V_i, Qd_i, Kd_i, dl_i = xs
        o = jnp.einsum('bhcd,bhde->bhce', Qd_i, S) + jnp.einsum('bhij,bhje->bhie', A_i, V_i)
        S = S * dl_i[..., None, None] + jnp.einsum('bhcd,bhce->bhde', Kd_i, V_i)
        return S, o

    S, o = lax.scan(step, S0, (Aqk, v, Qd, Kd, dlast))
    return _unchunk(o, l), S


def _rotary(x, pos):
    half = x.shape[-1] // 2
    inv = 1.0 / (ROPE_BASE ** (jnp.arange(half, dtype=F32) / half))
    ang = pos[:, None] * inv[None, :]
    cos = jnp.cos(ang)[None, :, None, :]
    sin = jnp.sin(ang)[None, :, None, :]
    x1, x2 = x[..., :half], x[..., half:]
    return jnp.concatenate([x1 * cos - x2 * sin, x1 * sin + x2 * cos], axis=-1)


def _l2norm(x):
    return x * lax.rsqrt(jnp.sum(x * x, axis=-1, keepdims=True) + EPS)


def _delta_mixer(z, s_dconv, s_delta, conv_w, a_log, dt_bias, norm_w):
    b, l, _ = z.shape
    H, dh = N_HEADS, HEAD_DIM
    qkv, new_buf = _causal_dwconv(z[..., 0:768], s_dconv, conv_w)
    qkv = jax.nn.silu(qkv)
    q = _l2norm(qkv[..., :GROUP_WIDTH].reshape(b, l, H, dh)) * (dh ** -0.5)
    k = _l2norm(qkv[..., GROUP_WIDTH:2 * GROUP_WIDTH].reshape(b, l, H, dh))
    v = qkv[..., 2 * GROUP_WIDTH:].reshape(b, l, H, dh)
    a = z[..., Z_AB:Z_AB + H]
    beta = jax.nn.sigmoid(z[..., Z_AB + H:Z_AB + 2 * H])
    g = -jnp.exp(a_log) * jax.nn.softplus(a + dt_bias)
    o, S = _gated_delta_rule(q, k, v, beta, g, s_delta)
    o = o * lax.rsqrt(jnp.mean(o * o, axis=-1, keepdims=True) + EPS) * norm_w
    o = o.reshape(b, l, GROUP_WIDTH) * jax.nn.silu(z[..., 768:1024])
    return o, new_buf, S


def _pool_mixer(z, s_pool, start_pos, pool_w, pool_scale):
    u = z[..., 1024:1280]
    b, l, _ = u.shape
    full = jnp.concatenate([s_pool, u], axis=1)
    new_buf = full[:, l:]
    cs = jnp.concatenate([jnp.zeros((b, 1, GROUP_WIDTH), F32), jnp.cumsum(full, axis=1)], axis=1)
    t = jnp.arange(l)
    means = []
    for gi, w in enumerate(POOL_WINDOWS):
        sl = slice(gi * POOL_GROUP, (gi + 1) * POOL_GROUP)
        wsum = cs[:, POOL_BUF + 1:POOL_BUF + 1 + l, sl] - cs[:, POOL_BUF + 1 - w:POOL_BUF + 1 - w + l, sl]
        cnt = jnp.minimum(w, start_pos + t + 1).astype(F32)
        means.append(wsum / cnt[None, :, None])
    p = (jnp.concatenate(means, axis=-1) - full[:, POOL_BUF:]).reshape(b, l, len(POOL_WINDOWS), POOL_GROUP)
    y = jnp.einsum('blgc,gcd->blgd', p, pool_w).reshape(b, l, GROUP_WIDTH)
    return y * pool_scale, new_buf


def _retention_mixer(z, s_ret, start_pos):
    b, l, _ = z.shape
    H, dh = N_HEADS, HEAD_DIM
    q = z[..., 1280:1536].reshape(b, l, H, dh)
    k = z[..., 1536:1792].reshape(b, l, H, dh)
    v = z[..., 1792:2048].reshape(b, l, H, dh)
    gate = z[..., 2048:2304]
    pos = jnp.arange(l, dtype=F32) + start_pos
    q = _rotary(q, pos)
    k = _rotary(k, pos) * (dh ** -0.5)
    log_gamma = jnp.log1p(-jnp.exp2(-5.0 - jnp.arange(H, dtype=F32)))
    g = jnp.broadcast_to(log_gamma, (b, l, H))
    o, S = _decayed_linear_attn(q, k, v, g, s_ret)
    mu = jnp.mean(o, axis=-1, keepdims=True)
    var = jnp.mean(jnp.square(o - mu), axis=-1, keepdims=True)
    o = (o - mu) * lax.rsqrt(var + EPS)
    return jax.nn.silu(gate) * o.reshape(b, l, GROUP_WIDTH), S


def _conformer_mixer(z, s_conv, dw_w, dw_b, ln_w, ln_b, pw_w):
    a = z[..., 2304:2560]
    gt = z[..., 2560:2816]
    glu = a * jax.nn.sigmoid(gt)
    dc, new_buf = _causal_dwconv(glu, s_conv, dw_w)
    dc = dc + dw_b
    mu = jnp.mean(dc, axis=-1, keepdims=True)
    var = jnp.mean(jnp.square(dc - mu), axis=-1, keepdims=True)
    hn = (dc - mu) * lax.rsqrt(var + EPS) * ln_w + ln_b
    return jnp.einsum('blc,cd->bld', jax.nn.silu(hn), pw_w), new_buf


ROW_CHUNK = 256
Z_COL = {'dz': 3, 'pu': 4, 'rq': 5, 'rk': 6, 'rv': 7, 'rg': 8, 'ca': 9, 'cg': 10}


def _zspec(name, l):
    return pl.BlockSpec((1, l, GROUP_WIDTH), lambda b, c=Z_COL[name]: (b, 0, c))


def _full(shape):
    return pl.BlockSpec(shape, lambda b: (0,) * len(shape))


def _split3(x):
    hi = x.astype(BF16)
    r = x - hi.astype(F32)
    mid = r.astype(BF16)
    lo = (r - mid.astype(F32)).astype(BF16)
    return hi, mid, lo


def _dot3(x, m_bf):
    hi, mid, lo = _split3(x)
    return _dot(hi, m_bf, 1, 0) + _dot(mid, m_bf, 1, 0) + _dot(lo, m_bf, 1, 0)


def _conf_kernel(a_ref, gt_ref, hist_ref, dw_ref, db_ref, lnw_ref, lnb_ref, pw_ref, o_ref, nh_ref, buf):
    l = a_ref.shape[1]
    off = 32
    buf[0:off - (CONF_CONV - 1), :] = jnp.zeros((off - (CONF_CONV - 1), GROUP_WIDTH), F32)
    buf[off - (CONF_CONV - 1):off, :] = hist_ref[0]
    buf[off:off + l, :] = a_ref[0] * jax.nn.sigmoid(gt_ref[0])
    for r in range(l // ROW_CHUNK):
        base = r * ROW_CHUNK + off - (CONF_CONV - 1)
        acc = jnp.zeros((ROW_CHUNK, GROUP_WIDTH), F32) + db_ref[...]
        for w in range(CONF_CONV):
            acc = acc + buf[base + w:base + w + ROW_CHUNK, :] * dw_ref[w:w + 1, :]
        mu = jnp.mean(acc, axis=-1, keepdims=True)
        d = acc - mu
        var = jnp.mean(d * d, axis=-1, keepdims=True)
        hn = d * lax.rsqrt(var + EPS) * lnw_ref[...] + lnb_ref[...]
        act = hn * jax.nn.sigmoid(hn)
        o_ref[0, r * ROW_CHUNK:(r + 1) * ROW_CHUNK, :] = _dot(act.astype(BF16), pw_ref[...], 1, 0)
    nh_ref[0] = buf[l + off - (CONF_CONV - 1):l + off, :]


def _conf_pallas(z, hist, dw, db, lnw, lnb, pw_bf):
    b, l, _ = z.shape
    c = GROUP_WIDTH
    row = lambda v: v.reshape(1, c)
    return pl.pallas_call(
        _conf_kernel,
        grid=(b,),
        in_specs=[_zspec('ca', l), _zspec('cg', l), pl.BlockSpec((1, CONF_CONV - 1, c), lambda i: (i, 0, 0)),
                  _full((CONF_CONV, c)), _full((1, c)), _full((1, c)), _full((1, c)), _full((c, c))],
        out_specs=[pl.BlockSpec((1, l, c), lambda i: (i, 0, 0)), pl.BlockSpec((1, CONF_CONV - 1, c), lambda i: (i, 0, 0))],
        out_shape=[jax.ShapeDtypeStruct((b, l, c), F32), jax.ShapeDtypeStruct((b, CONF_CONV - 1, c), F32)],
        scratch_shapes=[pltpu.VMEM((l + 32, c), F32)],
        compiler_params=_cparams("arbitrary"),
        name="conformer_mixer",
    )(z, z, hist, dw, row(db), row(lnw), row(lnb), pw_bf)


def _pool_kernel(u_ref, hist_ref, wbd_ref, scale_ref, o_ref, nh_ref, buf, *, start_pos):
    l = u_ref.shape[1]
    off = 16
    buf[0:off - POOL_BUF, :] = jnp.zeros((off - POOL_BUF, GROUP_WIDTH), F32)
    buf[off - POOL_BUF:off, :] = hist_ref[0]
    buf[off:off + l, :] = u_ref[0]
    lane = lax.broadcasted_iota(jnp.int32, (ROW_CHUNK, GROUP_WIDTH), 1)
    rowi = lax.broadcasted_iota(jnp.int32, (ROW_CHUNK, GROUP_WIDTH), 0)
    for r in range(l // ROW_CHUNK):
        base = r * ROW_CHUNK + off
        avail = (rowi + (r * ROW_CHUNK + start_pos + 1)).astype(F32)
        cur = buf[base:base + ROW_CHUNK, :]
        s = cur
        mean = None
        for back in range(1, max(POOL_WINDOWS)):
            s = s + buf[base - back:base - back + ROW_CHUNK, :]
            if back + 1 in POOL_WINDOWS:
                gi = POOL_WINDOWS.index(back + 1)
                m = s / jnp.minimum(float(back + 1), avail)
                mean = m if mean is None else jnp.where(lane >= gi * POOL_GROUP, m, mean)
        p = mean - cur
        y = _dot(p.astype(BF16), wbd_ref[...], 1, 0) * scale_ref[...]
        o_ref[0, r * ROW_CHUNK:(r + 1) * ROW_CHUNK, :] = y
    nh_ref[0] = buf[l + off - POOL_BUF:l + off, :]


def _pool_pallas(z, hist, wbd_bf, scale, start_pos):
    b, l, _ = z.shape
    c = GROUP_WIDTH
    return pl.pallas_call(
        functools.partial(_pool_kernel, start_pos=start_pos),
        grid=(b,),
        in_specs=[_zspec('pu', l), pl.BlockSpec((1, POOL_BUF, c), lambda i: (i, 0, 0)), _full((c, c)), _full((1, c))],
        out_specs=[pl.BlockSpec((1, l, c), lambda i: (i, 0, 0)), pl.BlockSpec((1, POOL_BUF, c), lambda i: (i, 0, 0))],
        out_shape=[jax.ShapeDtypeStruct((b, l, c), F32), jax.ShapeDtypeStruct((b, POOL_BUF, c), F32)],
        scratch_shapes=[pltpu.VMEM((l + 16, c), F32)],
        compiler_params=_cparams("arbitrary"),
        name="pool_mixer",
    )(z, hist, wbd_bf, scale.reshape(1, c))


def _ret_kernel(q_ref, k_ref, v_ref, g_ref, cos_ref, sin_ref, perm_ref, dmask_ref, eg_ref, kd_ref, dl_ref, bd_ref,
                s0_ref, o_ref, sn_ref, o_scr, s_scr):
    l = q_ref.shape[1]
    hd = HEAD_DIM
    s_scr[...] = s0_ref[0]

    def per_chunk(ci, carry):
        rows = pl.ds(pl.multiple_of(ci * CHUNK, CHUNK), CHUNK)
        cos = cos_ref[rows, :]
        sin = sin_ref[rows, :]
        q = q_ref[0, rows, :]
        k = k_ref[0, rows, :]
        q = q * cos + _dot3(q, perm_ref[...]) * sin
        k = (k * cos + _dot3(k, perm_ref[...]) * sin) * (hd ** -0.5)
        v = v_ref[0, rows, :]
        for h in range(N_HEADS):
            ls = slice(h * hd, (h + 1) * hd)
            qh, kh, vh = q[:, ls], k[:, ls], v[:, ls].astype(BF16)
            s_h = s_scr[h]
            aqk = _dot(qh.astype(BF16), kh.astype(BF16), 1, 1) * dmask_ref[h]
            o = _dot((qh * eg_ref[h]).astype(BF16), s_h.astype(BF16), 1, 0) + _dot(aqk.astype(BF16), vh, 1, 0)
            s_scr[h] = s_h * dl_ref[h] + _dot((kh * kd_ref[h]).astype(BF16), vh, 0, 0)
            o_scr[rows, ls] = o
        return carry

    lax.fori_loop(0, l // CHUNK, per_chunk, 0)
    sn_ref[0] = s_scr[...]
    for r in range(l // ROW_CHUNK):
        rows = slice(r * ROW_CHUNK, (r + 1) * ROW_CHUNK)
        o = o_scr[rows, :]
        d = o - _dot3(o, bd_ref[...])
        var = _dot3(d * d, bd_ref[...])
        g = g_ref[0, rows, :]
        o_ref[0, rows, :] = g * jax.nn.sigmoid(g) * (d * lax.rsqrt(var + EPS))


def _ret_tables(l, start_pos):
    hd, half = HEAD_DIM, HEAD_DIM // 2
    pos = jnp.arange(l, dtype=F32) + start_pos
    inv = 1.0 / (ROPE_BASE ** (jnp.arange(half, dtype=F32) / half))
    ang = pos[:, None] * inv[None, :]
    cos = jnp.tile(jnp.cos(ang), (1, 2 * N_HEADS))
    sin = jnp.tile(jnp.sin(ang), (1, 2 * N_HEADS))
    lane = np.arange(GROUP_WIDTH)
    perm = np.zeros((GROUP_WIDTH, GROUP_WIDTH), np.float32)
    first = (lane % hd) < half
    perm[lane[first] + half, lane[first]] = -1.0
    perm[lane[~first] - half, lane[~first]] = 1.0
    bd = (lane[:, None] // hd == lane[None, :] // hd).astype(np.float32) / hd
    c = min(CHUNK, l)
    log_gamma = jnp.log1p(-jnp.exp2(-5.0 - jnp.arange(N_HEADS, dtype=F32)))
    G = jnp.cumsum(jnp.broadcast_to(log_gamma[:, None], (N_HEADS, c)), axis=-1)
    idx = jnp.arange(c)
    incl = idx[:, None] >= idx[None, :]
    dmask = jnp.exp(jnp.where(incl, G[:, :, None] - G[:, None, :], -jnp.inf))
    ones = jnp.ones((N_HEADS, c, hd), F32)
    eg = jnp.exp(G)[:, :, None] * ones
    kd = jnp.exp(G[:, -1:] - G)[:, :, None] * ones
    dl = jnp.exp(G[:, -1])[:, None, None] * jnp.ones((N_HEADS, hd, hd), F32)
    return cos, sin, jnp.asarray(perm, BF16), dmask, eg, kd, dl, jnp.asarray(bd, BF16)


def _ret_pallas(z, s0, start_pos):
    b, l, _ = z.shape
    c = GROUP_WIDTH
    tabs = _ret_tables(l, start_pos)
    st_spec = pl.BlockSpec((1, N_HEADS, HEAD_DIM, HEAD_DIM), lambda i: (i, 0, 0, 0))
    return pl.pallas_call(
        _ret_kernel,
        grid=(b,),
        in_specs=[_zspec('rq', l), _zspec('rk', l), _zspec('rv', l), _zspec('rg', l)]
                 + [_full(t.shape) for t in tabs] + [st_spec],
        out_specs=[pl.BlockSpec((1, l, c), lambda i: (i, 0, 0)), st_spec],
        out_shape=[jax.ShapeDtypeStruct((b, l, c), F32), jax.ShapeDtypeStruct(s0.shape, F32)],
        scratch_shapes=[pltpu.VMEM((l, c), F32), pltpu.VMEM((N_HEADS, HEAD_DIM, HEAD_DIM), F32)],
        compiler_params=_cparams("arbitrary"),
        name="retention_mixer",
    )(z, z, z, z, *tabs, s0)


def _pool_blockdiag(pool_w):
    out = jnp.zeros((GROUP_WIDTH, GROUP_WIDTH), pool_w.dtype)
    for gi in range(len(POOL_WINDOWS)):
        sl = slice(gi * POOL_GROUP, (gi + 1) * POOL_GROUP)
        out = out.at[sl, sl].set(pool_w[gi])
    return out


def _permute_w_in(w_in):
    pad = jnp.zeros((w_in.shape[0], Z_WIDTH - Z_AB - 8), w_in.dtype)
    return jnp.concatenate([w_in[:, :OFF_DA], w_in[:, OFF_PU:], w_in[:, OFF_DA:OFF_PU], pad], axis=1)


def _run_trunk(x, states, params, final_norm, start_pos):
    b, l, _ = x.shape
    n = b * l
    xf = x.reshape(n, D_MODEL)
    s_delta, s_dconv, s_pool, s_ret, s_conv = states
    new = ([], [], [], [], [])
    for li in range(DEPTH):
        p = {name: arr[li] for name, arr in params.items()}
        z = _in_proj(xf, p['norm1'].reshape(1, D_MODEL), p['w_in_bf']).reshape(b, l, Z_WIDTH)
        o_a, n_dconv, n_delta = _delta_mixer(z, s_dconv[li], s_delta[li], p['delta_conv_w'], p['delta_a_log'],
                                             p['delta_dt_bias'], p['delta_norm_w'])
        if l % ROW_CHUNK == 0:
            o_b, n_pool = _pool_pallas(z, s_pool[li], p['pool_wbd_bf'], p['pool_scale'], start_pos)
            o_c, n_ret = _ret_pallas(z, s_ret[li], start_pos)
            o_d, n_conv = _conf_pallas(z, s_conv[li], p['conv_dw_w'], p['conv_dw_b'], p['conv_ln_w'],
                                       p['conv_ln_b'], p['conv_pw_bf'])
        else:
            o_b, n_pool = _pool_mixer(z, s_pool[li], start_pos, p['pool_w'], p['pool_scale'])
            o_c, n_ret = _retention_mixer(z, s_ret[li], start_pos)
            o_d, n_conv = _conformer_mixer(z, s_conv[li], p['conv_dw_w'], p['conv_dw_b'], p['conv_ln_w'],
                                           p['conv_ln_b'], p['conv_pw_w'])
        mix = jnp.concatenate([o_a, o_b, o_c, o_d], axis=-1).reshape(n, D_MODEL)
        xf, hn = _out_proj(xf, mix, p['w_out_bf'], p['norm2'].reshape(1, D_MODEL))
        xf = peer_block(hn, xf, p['wqt_bf'], p['keys_bf'], p['u_bf'], p['v_bf'], final_norm.reshape(1, D_MODEL),
                        li == DEPTH - 1)
        for acc, a in zip(new, (n_delta, n_dconv, n_pool, n_ret, n_conv)):
            acc.append(a)
    return xf.reshape(b, l, D_MODEL), tuple(jnp.stack(acc) for acc in new)


def kernel(x_prompt, x_sample, state_delta, state_delta_conv, state_pool, state_ret, state_conv, norm1, w_in, delta_conv_w, delta_a_log, delta_dt_bias, delta_norm_w, pool_w, pool_scale, conv_dw_w, conv_dw_b, conv_ln_w, conv_ln_b, conv_pw_w, w_out, norm2, peer_wq, peer_keys, peer_u, peer_v, final_norm):
    params = {
        'norm1': norm1, 'delta_conv_w': delta_conv_w, 'delta_a_log': delta_a_log, 'delta_dt_bias': delta_dt_bias,
        'delta_norm_w': delta_norm_w, 'pool_w': pool_w, 'pool_scale': pool_scale, 'conv_dw_w': conv_dw_w,
        'conv_dw_b': conv_dw_b, 'conv_ln_w': conv_ln_w, 'conv_ln_b': conv_ln_b, 'conv_pw_w': conv_pw_w,
        'norm2': norm2,
        'w_in_bf': jax.vmap(_permute_w_in)(w_in).astype(BF16),
        'w_out_bf': w_out.astype(BF16),
        'pool_wbd_bf': jax.vmap(_pool_blockdiag)(pool_w).astype(BF16),
        'conv_pw_bf': conv_pw_w.astype(BF16),
        'wqt_bf': jnp.swapaxes(peer_wq, 1, 2).astype(BF16),
        'keys_bf': peer_keys.reshape(DEPTH, 2 * PEER_HEADS * PEER_NKEYS, PEER_DHALF).astype(BF16),
        'u_bf': peer_u.astype(BF16),
        'v_bf': peer_v.astype(BF16),
    }
    bp = x_prompt.shape[0]
    zeros = tuple(jnp.zeros((DEPTH, bp) + s.shape[2:], F32)
                  for s in (state_delta, state_delta_conv, state_pool, state_ret, state_conv))
    y_p, st_p = _run_trunk(x_prompt, zeros, params, final_norm, 0)
    y_s, st_s = _run_trunk(x_sample, (state_delta, state_delta_conv, state_pool, state_ret, state_conv),
                           params, final_norm, PAST_LEN)
    return (y_p, y_s) + st_p + st_s
```

```python
import functools

import numpy as np
import jax
import jax.numpy as jnp
from jax import lax
from jax.experimental import pallas as pl
from jax.experimental.pallas import tpu as pltpu

F32 = jnp.float32
BF16 = jnp.bfloat16

D_MODEL = 1024
DEPTH = 2
HEAD_DIM = 64
GROUP_WIDTH = 256
N_HEADS = 4
POOL_WINDOWS = (2, 4, 8, 16)
POOL_GROUP = 64
POOL_BUF = 15
DELTA_CONV = 4
CONF_CONV = 31
CHUNK = 64
ROPE_BASE = 10000.0
EPS = 1e-6
PAST_LEN = 16384

OFF_DQKV = 0
OFF_DZ = 768
OFF_DA = 1024
OFF_DB = 1028
OFF_PU = 1032
OFF_RQ = 1288
OFF_RK = 1544
OFF_RV = 1800
OFF_RG = 2056
OFF_CG = 2312
IN_WIDTH = 2824
Z_AB = 2816
Z_WIDTH = 2944

PEER_HEADS = 8
PEER_NKEYS = 128
PEER_TOPK = 16
PEER_DHALF = 128
PEER_EXPERTS = PEER_NKEYS * PEER_NKEYS

VMEM_LIMIT = 56 * 1024 * 1024

NEG_INF = float("-inf")
NOT_RANKED = 99.0

PEER_CANDS = tuple((k, l) for k in range(1, PEER_TOPK + 1) for l in range(1, PEER_TOPK + 1) if k * l <= PEER_TOPK)


def _cparams(*sem):
    return pltpu.CompilerParams(dimension_semantics=sem, vmem_limit_bytes=VMEM_LIMIT)


def _dot(a, b, ca, cb, precision=None):
    return lax.dot_general(a, b, (((ca,), (cb,)), ((), ())), preferred_element_type=F32, precision=precision)


def _in_proj_kernel(x_ref, nw_ref, w_ref, z_ref):
    x = x_ref[...]
    h = x * lax.rsqrt(jnp.mean(x * x, axis=-1, keepdims=True) + EPS) * nw_ref[...]
    z_ref[...] = _dot(h.astype(BF16), w_ref[...], 1, 0)


def _in_proj(x, nw, w_bf):
    n = x.shape[0]
    tm = min(256, n)
    return pl.pallas_call(
        _in_proj_kernel,
        grid=(n // tm,),
        in_specs=[pl.BlockSpec((tm, D_MODEL), lambda i: (i, 0)),
                  pl.BlockSpec((1, D_MODEL), lambda i: (0, 0)),
                  pl.BlockSpec(w_bf.shape, lambda i: (0, 0))],
        out_specs=pl.BlockSpec((tm, w_bf.shape[1]), lambda i: (i, 0)),
        out_shape=jax.ShapeDtypeStruct((n, w_bf.shape[1]), F32),
        compiler_params=_cparams("arbitrary"),
        name="in_proj",
    )(x, nw, w_bf)


def _out_proj_kernel(x_ref, mix_ref, w_ref, nw_ref, xo_ref, hn_ref):
    y = x_ref[...] + _dot(mix_ref[...].astype(BF16), w_ref[...], 1, 0)
    xo_ref[...] = y
    hn = y * lax.rsqrt(jnp.mean(y * y, axis=-1, keepdims=True) + EPS) * nw_ref[...]
    hn_ref[...] = hn.astype(BF16)


def _out_proj(x, mix, w_bf, nw):
    n = x.shape[0]
    tm = min(256, n)
    tok = pl.BlockSpec((tm, D_MODEL), lambda i: (i, 0))
    return pl.pallas_call(
        _out_proj_kernel,
        grid=(n // tm,),
        in_specs=[tok, tok, pl.BlockSpec(w_bf.shape, lambda i: (0, 0)), pl.BlockSpec((1, D_MODEL), lambda i: (0, 0))],
        out_specs=[tok, tok],
        out_shape=[jax.ShapeDtypeStruct((n, D_MODEL), F32), jax.ShapeDtypeStruct((n, D_MODEL), BF16)],
        compiler_params=_cparams("arbitrary"),
        name="out_proj",
    )(x, mix, w_bf, nw)


def _topk_ranks(s):
    nk, t = s.shape
    iota = lax.broadcasted_iota(jnp.int32, (nk, t), 0).astype(F32)
    kiota = lax.broadcasted_iota(jnp.int32, (PEER_TOPK, t), 0)

    def extract(k, carry):
        s, r, tops = carry
        m = jnp.max(s, axis=0, keepdims=True)
        idx = jnp.min(jnp.where(s == m, iota, float(nk)), axis=0, keepdims=True)
        one = iota == idx
        tops = jnp.where(kiota == k, m, tops)
        return jnp.where(one, NEG_INF, s), jnp.where(one, lax.convert_element_type(k + 1, F32), r), tops

    init = (s, jnp.full((nk, t), NOT_RANKED, F32), jnp.zeros((PEER_TOPK, t), F32))
    _, r, tops = lax.fori_loop(0, PEER_TOPK, extract, init)
    return tops, r


def _select_pairs(a, b):
    cand = [a[k - 1] + b[l - 1] for (k, l) in PEER_CANDS]
    sel = [jnp.zeros_like(cand[0]) for _ in PEER_CANDS]
    zsum = jnp.zeros_like(cand[0])
    m0 = functools.reduce(jnp.maximum, cand)
    for _ in range(PEER_TOPK):
        m = functools.reduce(jnp.maximum, cand)
        zsum = zsum + jnp.exp(m - m0)
        found = jnp.zeros_like(m)
        for ci in range(len(PEER_CANDS)):
            hit = jnp.where(cand[ci] == m, 1.0 - found, 0.0)
            found = found + hit
            sel[ci] = sel[ci] + hit
            cand[ci] = jnp.where(hit > 0.0, NEG_INF, cand[ci])
    height = []
    for k in range(1, PEER_TOPK + 1):
        hk = jnp.zeros_like(zsum)
        for ci, (kk, _) in enumerate(PEER_CANDS):
            if kk == k:
                hk = hk + sel[ci]
        height.append(hk)
    return height, 1.0 / zsum


def _peer_a_kernel(hn_ref, wqt_ref, keys_ref, e0_ref, lrow_ref, r1_ref, e1_ref, q_scr, s_scr, top_scr):
    ta = hn_ref.shape[0]
    nk = PEER_NKEYS
    q_scr[...] = _dot(wqt_ref[...], hn_ref[...], 1, 1).astype(BF16)

    for hp in range(2 * PEER_HEADS):
        h, p = hp // 2, hp % 2
        hp_rows = slice(hp * nk, (hp + 1) * nk)
        s_all = _dot(keys_ref[hp_rows, :], q_scr[hp_rows, :], 1, 0)
        s_scr[hp_rows, :] = s_all
        for lt in range(ta // 128):
            lanes = slice(lt * 128, (lt + 1) * 128)
            tops, r = _topk_ranks(s_all[:, lanes])
            for k in range(PEER_TOPK):
                top_scr[p, k, h:h + 1, lanes] = tops[k:k + 1, :]
            (lrow_ref if p == 0 else r1_ref)[h * nk:(h + 1) * nk, lanes] = r

    a = [top_scr[0, k] for k in range(PEER_TOPK)]
    b = [top_scr[1, k] for k in range(PEER_TOPK)]
    height, inv_z = _select_pairs(a, b)
    for h in range(PEER_HEADS):
        rows = slice(h * nk, (h + 1) * nk)
        r0 = lrow_ref[rows, :]
        lrow = jnp.zeros_like(r0)
        for k in range(1, PEER_TOPK + 1):
            lrow = jnp.where(r0 == float(k), height[k - 1][h:h + 1, :], lrow)
        lrow_ref[rows, :] = lrow
        s0 = s_scr[2 * h * nk:(2 * h + 1) * nk, :]
        s1 = s_scr[(2 * h + 1) * nk:(2 * h + 2) * nk, :]
        e0_ref[rows, :] = jnp.exp(s0 - a[0][h:h + 1, :]) * inv_z[h:h + 1, :]
        e1_ref[rows, :] = jnp.exp(s1 - b[0][h:h + 1, :])


def _peer_a(hn, wqt, keys2, ta):
    n = hn.shape[0]
    rows = PEER_HEADS * PEER_NKEYS
    out = jax.ShapeDtypeStruct((rows, n), F32)
    fac_spec = pl.BlockSpec((rows, ta), lambda t: (0, t))
    return pl.pallas_call(
        _peer_a_kernel,
        grid=(n // ta,),
        in_specs=[
            pl.BlockSpec((ta, D_MODEL), lambda t: (t, 0)),
            pl.BlockSpec(wqt.shape, lambda t: (0, 0)),
            pl.BlockSpec(keys2.shape, lambda t: (0, 0)),
        ],
        out_specs=[fac_spec] * 4,
        out_shape=[out] * 4,
        scratch_shapes=[
            pltpu.VMEM((2 * rows, ta), BF16),
            pltpu.VMEM((2 * rows, ta), F32),
            pltpu.VMEM((2, PEER_TOPK, PEER_HEADS, ta), F32),
        ],
        compiler_params=_cparams("arbitrary"),
        name="peer_topk",
    )(hn, wqt, keys2)


def _gelu_tanh(x):
    return 0.5 * x * (1.0 + jnp.tanh(0.7978845608028654 * (x + 0.044715 * (x * x * x))))


def _peer_b_kernel(hn_ref, x_ref, e0_ref, lrow_ref, r1_ref, e1_ref, u_ref, v_ref, fw_ref, o_ref, acc_ref, w_scr,
                   *, rows_per_chunk, final_norm):
    c = pl.program_id(1)
    nk = PEER_NKEYS
    t = hn_ref.shape[0]

    @pl.when(c == 0)
    def _():
        acc_ref[...] = jnp.zeros_like(acc_ref)

    w_scr[...] = _gelu_tanh(_dot(u_ref[...], hn_ref[...], 1, 1))

    def per_row(ii, carry):
        i = c * rows_per_chunk + ii
        lrow = [jnp.broadcast_to(lrow_ref[pl.ds(h * nk + i, 1), :], (8, t)) for h in range(PEER_HEADS)]
        e0 = [jnp.broadcast_to(e0_ref[pl.ds(h * nk + i, 1), :], (8, t)) for h in range(PEER_HEADS)]
        for jt in range(nk // 8):
            g = jnp.zeros((8, t), F32)
            for h in range(PEER_HEADS):
                rows = slice(h * nk + jt * 8, h * nk + jt * 8 + 8)
                g = g + jnp.where(r1_ref[rows, :] <= lrow[h], e1_ref[rows, :], 0.0) * e0[h]
            wrow = pl.ds(pl.multiple_of(ii * nk + jt * 8, 8), 8)
            w_scr[wrow, :] = w_scr[wrow, :] * g
        return carry

    lax.fori_loop(0, rows_per_chunk, per_row, 0)
    acc_ref[...] += _dot(w_scr[...].astype(BF16), v_ref[...], 0, 0)

    @pl.when(c == pl.num_programs(1) - 1)
    def _():
        y = x_ref[...] + acc_ref[...]
        if final_norm:
            y = y * lax.rsqrt(jnp.mean(y * y, axis=-1, keepdims=True) + EPS) * fw_ref[...]
        o_ref[...] = y


def _peer_b(hn, x, facs, u_bf, v_bf, final_w, t, rows_per_chunk, final_norm):
    n = hn.shape[0]
    rows = PEER_HEADS * PEER_NKEYS
    ec = rows_per_chunk * PEER_NKEYS
    fac_spec = pl.BlockSpec((rows, t), lambda ti, c: (0, ti))
    tok_spec = pl.BlockSpec((t, D_MODEL), lambda ti, c: (ti, 0))
    exp_spec = pl.BlockSpec((ec, D_MODEL), lambda ti, c: (c, 0))
    return pl.pallas_call(
        functools.partial(_peer_b_kernel, rows_per_chunk=rows_per_chunk, final_norm=final_norm),
        grid=(n // t, PEER_EXPERTS // ec),
        in_specs=[tok_spec, tok_spec, fac_spec, fac_spec, fac_spec, fac_spec, exp_spec, exp_spec,
                  pl.BlockSpec((1, D_MODEL), lambda ti, c: (0, 0))],
        out_specs=tok_spec,
        out_shape=jax.ShapeDtypeStruct((n, D_MODEL), F32),
        scratch_shapes=[pltpu.VMEM((t, D_MODEL), F32), pltpu.VMEM((ec, t), F32)],
        compiler_params=_cparams("arbitrary", "arbitrary"),
        name="peer_experts",
    )(hn, x, *facs, u_bf, v_bf, final_w)


def peer_block(hn, x, wqt, keys2, u_bf, v_bf, final_w, final_norm):
    n = hn.shape[0]
    facs = _peer_a(hn, wqt, keys2, min(256, n))
    return _peer_b(hn, x, facs, u_bf, v_bf, final_w, min(512, n), 8, final_norm)


def _causal_dwconv(u, buf, w):
    full = jnp.concatenate([buf.astype(u.dtype), u], axis=1)
    y = lax.conv_general_dilated(full, w[:, None, :].astype(u.dtype), window_strides=(1,), padding='VALID',
                                 dimension_numbers=('NWC', 'WIO', 'NWC'), feature_group_count=u.shape[-1])
    return y, full[:, full.shape[1] - (w.shape[0] - 1):]


def _chunk(x, c):
    b, l = x.shape[:2]
    n = -(-l // c)
    x = jnp.pad(x, [(0, 0), (0, n * c - l)] + [(0, 0)] * (x.ndim - 2))
    x = x.reshape((b, n, c) + x.shape[2:])
    return jnp.transpose(x, (1, 0, 3, 2) + tuple(range(4, x.ndim)))


def _unchunk(o, l):
    n, b, h, c, d = o.shape
    return jnp.transpose(o, (1, 0, 3, 2, 4)).reshape(b, n * c, h, d)[:, :l]


def _decay_masks(g, c):
    G = jnp.cumsum(g, axis=-1)
    idx = jnp.arange(c)
    incl = idx[:, None] >= idx[None, :]
    diff = G[..., :, None] - G[..., None, :]
    dmask = jnp.exp(jnp.where(incl, diff, -jnp.inf))
    return G, dmask, idx


def _gated_delta_rule(q, k, v, beta, g, S0):
    l = q.shape[1]
    c = min(CHUNK, l)
    q, k, v, beta, g = (_chunk(t, c) for t in (q, k, v, beta, g))
    G, dmask, idx = _decay_masks(g, c)
    strict = idx[:, None] > idx[None, :]
    kb = k * beta[..., None]
    A = jnp.where(strict, jnp.einsum('...id,...jd->...ij', kb, k) * dmask, 0.0)
    eye = jnp.eye(c, dtype=A.dtype)
    T = lax.linalg.triangular_solve(eye + A, jnp.broadcast_to(eye, A.shape), left_side=True, lower=True)
    eG = jnp.exp(G)[..., None]
    U = jnp.einsum('...ij,...jd->...id', T, v * beta[..., None])
    W = jnp.einsum('...ij,...jd->...id', T, kb * eG)
    Qd = q * eG
    Aqk = jnp.einsum('...id,...jd->...ij', q, k) * dmask
    Glast = G[..., -1:]
    Kd = k * jnp.exp(Glast - G)[..., None]
    dlast = jnp.exp(Glast[..., 0])

    def step(S, xs):
        U_i, W_i, Qd_i, A_i, Kd_i, dl_i = xs
        v_new = U_i - jnp.einsum('bhcd,bhde->bhce', W_i, S)
        o = jnp.einsum('bhcd,bhde->bhce', Qd_i, S) + jnp.einsum('bhij,bhje->bhie', A_i, v_new)
        S = S * dl_i[..., None, None] + jnp.einsum('bhcd,bhce->bhde', Kd_i, v_new)
        return S, o

    S, o = lax.scan(step, S0, (U, W, Qd, Aqk, Kd, dlast))
    return _unchunk(o, l), S


def _decayed_linear_attn(q, k, v, g, S0):
    l = q.shape[1]
    c = min(CHUNK, l)
    q, k, v, g = (_chunk(t, c) for t in (q, k, v, g))
    G, dmask, _ = _decay_masks(g, c)
    Aqk = jnp.einsum('...id,...jd->...ij', q, k) * dmask
    Qd = q * jnp.exp(G)[..., None]
    Glast = G[..., -1:]
    Kd = k * jnp.exp(Glast - G)[..., None]
    dlast = jnp.exp(Glast[..., 0])

    def step(S, xs):
        A_i, V_i, Qd_i, Kd_i, dl_i = xs
        o = jnp.einsum('bhcd,bhde->bhce', Qd_i, S) + jnp.einsum('bhij,bhje->bhie', A_i, V_i)
        S = S * dl_i[..., None, None] + jnp.einsum('bhcd,bhce->bhde', Kd_i, V_i)
        return S, o

    S, o = lax.scan(step, S0, (Aqk, v, Qd, Kd, dlast))
    return _unchunk(o, l), S


def _rotary(x, pos):
    half = x.shape[-1] // 2
    inv = 1.0 / (ROPE_BASE ** (jnp.arange(half, dtype=F32) / half))
    ang = pos[:, None] * inv[None, :]
    cos = jnp.cos(ang)[None, :, None, :]
    sin = jnp.sin(ang)[None, :, None, :]
    x1, x2 = x[..., :half], x[..., half:]
    return jnp.concatenate([x1 * cos - x2 * sin, x1 * sin + x2 * cos], axis=-1)


def _l2norm(x):
    return x * lax.rsqrt(jnp.sum(x * x, axis=-1, keepdims=True) + EPS)


def _delta_mixer(z, s_dconv, s_delta, conv_w, a_log, dt_bias, norm_w):
    b, l, _ = z.shape
    H, dh = N_HEADS, HEAD_DIM
    qkv, new_buf = _causal_dwconv(z[..., 0:768], s_dconv, conv_w)
    qkv = jax.nn.silu(qkv)
    q = _l2norm(qkv[..., :GROUP_WIDTH].reshape(b, l, H, dh)) * (dh ** -0.5)
    k = _l2norm(qkv[..., GROUP_WIDTH:2 * GROUP_WIDTH].reshape(b, l, H, dh))
    v = qkv[..., 2 * GROUP_WIDTH:].reshape(b, l, H, dh)
    a = z[..., Z_AB:Z_AB + H]
    beta = jax.nn.sigmoid(z[..., Z_AB + H:Z_AB + 2 * H])
    g = -jnp.exp(a_log) * jax.nn.softplus(a + dt_bias)
    o, S = _gated_delta_rule(q, k, v, beta, g, s_delta)
    o = o * lax.rsqrt(jnp.mean(o * o, axis=-1, keepdims=True) + EPS) * norm_w
    o = o.reshape(b, l, GROUP_WIDTH) * jax.nn.silu(z[..., 768:1024])
    return o, new_buf, S


def _pool_mixer(z, s_pool, start_pos, pool_w, pool_scale):
    u = z[..., 1024:1280]
    b, l, _ = u.shape
    full = jnp.concatenate([s_pool, u], axis=1)
    new_buf = full[:, l:]
    cs = jnp.concatenate([jnp.zeros((b, 1, GROUP_WIDTH), F32), jnp.cumsum(full, axis=1)], axis=1)
    t = jnp.arange(l)
    means = []
    for gi, w in enumerate(POOL_WINDOWS):
        sl = slice(gi * POOL_GROUP, (gi + 1) * POOL_GROUP)
        wsum = cs[:, POOL_BUF + 1:POOL_BUF + 1 + l, sl] - cs[:, POOL_BUF + 1 - w:POOL_BUF + 1 - w + l, sl]
        cnt = jnp.minimum(w, start_pos + t + 1).astype(F32)
        means.append(wsum / cnt[None, :, None])
    p = (jnp.concatenate(means, axis=-1) - full[:, POOL_BUF:]).reshape(b, l, len(POOL_WINDOWS), POOL_GROUP)
    y = jnp.einsum('blgc,gcd->blgd', p, pool_w).reshape(b, l, GROUP_WIDTH)
    return y * pool_scale, new_buf


def _retention_mixer(z, s_ret, start_pos):
    b, l, _ = z.shape
    H, dh = N_HEADS, HEAD_DIM
    q = z[..., 1280:1536].reshape(b, l, H, dh)
    k = z[..., 1536:1792].reshape(b, l, H, dh)
    v = z[..., 1792:2048].reshape(b, l, H, dh)
    gate = z[..., 2048:2304]
    pos = jnp.arange(l, dtype=F32) + start_pos
    q = _rotary(q, pos)
    k = _rotary(k, pos) * (dh ** -0.5)
    log_gamma = jnp.log1p(-jnp.exp2(-5.0 - jnp.arange(H, dtype=F32)))
    g = jnp.broadcast_to(log_gamma, (b, l, H))
    o, S = _decayed_linear_attn(q, k, v, g, s_ret)
    mu = jnp.mean(o, axis=-1, keepdims=True)
    var = jnp.mean(jnp.square(o - mu), axis=-1, keepdims=True)
    o = (o - mu) * lax.rsqrt(var + EPS)
    return jax.nn.silu(gate) * o.reshape(b, l, GROUP_WIDTH), S


def _conformer_mixer(z, s_conv, dw_w, dw_b, ln_w, ln_b, pw_w):
    a = z[..., 2304:2560]
    gt = z[..., 2560:2816]
    glu = a * jax.nn.sigmoid(gt)
    dc, new_buf = _causal_dwconv(glu, s_conv, dw_w)
    dc = dc + dw_b
    mu = jnp.mean(dc, axis=-1, keepdims=True)
    var = jnp.mean(jnp.square(dc - mu), axis=-1, keepdims=True)
    hn = (dc - mu) * lax.rsqrt(var + EPS) * ln_w + ln_b
    return jnp.einsum('blc,cd->bld', jax.nn.silu(hn), pw_w), new_buf


ROW_CHUNK = 256
Z_COL = {'dz': 3, 'pu': 4, 'rq': 5, 'rk': 6, 'rv': 7, 'rg': 8, 'ca': 9, 'cg': 10}


def _zspec(name, l):
    return pl.BlockSpec((1, l, GROUP_WIDTH), lambda b, c=Z_COL[name]: (b, 0, c))


def _full(shape):
    return pl.BlockSpec(shape, lambda b: (0,) * len(shape))


def _split3(x):
    hi = x.astype(BF16)
    r = x - hi.astype(F32)
    mid = r.astype(BF16)
    lo = (r - mid.astype(F32)).astype(BF16)
    return hi, mid, lo


def _dot3(x, m_bf):
    hi, mid, lo = _split3(x)
    return _dot(hi, m_bf, 1, 0) + _dot(mid, m_bf, 1, 0) + _dot(lo, m_bf, 1, 0)


def _conf_kernel(a_ref, gt_ref, hist_ref, dw_ref, db_ref, lnw_ref, lnb_ref, pw_ref, o_ref, nh_ref, buf):
    l = a_ref.shape[1]
    off = 32
    buf[0:off - (CONF_CONV - 1), :] = jnp.zeros((off - (CONF_CONV - 1), GROUP_WIDTH), F32)
    buf[off - (CONF_CONV - 1):off, :] = hist_ref[0]
    buf[off:off + l, :] = a_ref[0] * jax.nn.sigmoid(gt_ref[0])
    for r in range(l // ROW_CHUNK):
        base = r * ROW_CHUNK + off - (CONF_CONV - 1)
        acc = jnp.zeros((ROW_CHUNK, GROUP_WIDTH), F32) + db_ref[...]
        for w in range(CONF_CONV):
            acc = acc + buf[base + w:base + w + ROW_CHUNK, :] * dw_ref[w:w + 1, :]
        mu = jnp.mean(acc, axis=-1, keepdims=True)
        d = acc - mu
        var = jnp.mean(d * d, axis=-1, keepdims=True)
        hn = d * lax.rsqrt(var + EPS) * lnw_ref[...] + lnb_ref[...]
        act = hn * jax.nn.sigmoid(hn)
        o_ref[0, r * ROW_CHUNK:(r + 1) * ROW_CHUNK, :] = _dot(act.astype(BF16), pw_ref[...], 1, 0)
    nh_ref[0] = buf[l + off - (CONF_CONV - 1):l + off, :]


def _conf_pallas(z, hist, dw, db, lnw, lnb, pw_bf):
    b, l, _ = z.shape
    c = GROUP_WIDTH
    row = lambda v: v.reshape(1, c)
    return pl.pallas_call(
        _conf_kernel,
        grid=(b,),
        in_specs=[_zspec('ca', l), _zspec('cg', l), pl.BlockSpec((1, CONF_CONV - 1, c), lambda i: (i, 0, 0)),
                  _full((CONF_CONV, c)), _full((1, c)), _full((1, c)), _full((1, c)), _full((c, c))],
        out_specs=[pl.BlockSpec((1, l, c), lambda i: (i, 0, 0)), pl.BlockSpec((1, CONF_CONV - 1, c), lambda i: (i, 0, 0))],
        out_shape=[jax.ShapeDtypeStruct((b, l, c), F32), jax.ShapeDtypeStruct((b, CONF_CONV - 1, c), F32)],
        scratch_shapes=[pltpu.VMEM((l + 32, c), F32)],
        compiler_params=_cparams("arbitrary"),
        name="conformer_mixer",
    )(z, z, hist, dw, row(db), row(lnw), row(lnb), pw_bf)


def _pool_kernel(u_ref, hist_ref, wbd_ref, scale_ref, o_ref, nh_ref, buf, *, start_pos):
    l = u_ref.shape[1]
    off = 16
    buf[0:off - POOL_BUF, :] = jnp.zeros((off - POOL_BUF, GROUP_WIDTH), F32)
    buf[off - POOL_BUF:off, :] = hist_ref[0]
    buf[off:off + l, :] = u_ref[0]
    lane = lax.broadcasted_iota(jnp.int32, (ROW_CHUNK, GROUP_WIDTH), 1)
    rowi = lax.broadcasted_iota(jnp.int32, (ROW_CHUNK, GROUP_WIDTH), 0)
    for r in range(l // ROW_CHUNK):
        base = r * ROW_CHUNK + off
        avail = (rowi + (r * ROW_CHUNK + start_pos + 1)).astype(F32)
        cur = buf[base:base + ROW_CHUNK, :]
        s = cur
        mean = None
        for back in range(1, max(POOL_WINDOWS)):
            s = s + buf[base - back:base - back + ROW_CHUNK, :]
            if back + 1 in POOL_WINDOWS:
                gi = POOL_WINDOWS.index(back + 1)
                m = s / jnp.minimum(float(back + 1), avail)
                mean = m if mean is None else jnp.where(lane >= gi * POOL_GROUP, m, mean)
        p = mean - cur
        y = _dot(p.astype(BF16), wbd_ref[...], 1, 0) * scale_ref[...]
        o_ref[0, r * ROW_CHUNK:(r + 1) * ROW_CHUNK, :] = y
    nh_ref[0] = buf[l + off - POOL_BUF:l + off, :]


def _pool_pallas(z, hist, wbd_bf, scale, start_pos):
    b, l, _ = z.shape
    c = GROUP_WIDTH
    return pl.pallas_call(
        functools.partial(_pool_kernel, start_pos=start_pos),
        grid=(b,),
        in_specs=[_zspec('pu', l), pl.BlockSpec((1, POOL_BUF, c), lambda i: (i, 0, 0)), _full((c, c)), _full((1, c))],
        out_specs=[pl.BlockSpec((1, l, c), lambda i: (i, 0, 0)), pl.BlockSpec((1, POOL_BUF, c), lambda i: (i, 0, 0))],
        out_shape=[jax.ShapeDtypeStruct((b, l, c), F32), jax.ShapeDtypeStruct((b, POOL_BUF, c), F32)],
        scratch_shapes=[pltpu.VMEM((l + 16, c), F32)],
        compiler_params=_cparams("arbitrary"),
        name="pool_mixer",
    )(z, hist, wbd_bf, scale.reshape(1, c))


def _ret_kernel(q_ref, k_ref, v_ref, g_ref, cos_ref, sin_ref, perm_ref, dmask_ref, eg_ref, kd_ref, dl_ref, bd_ref,
                s0_ref, o_ref, sn_ref, o_scr, s_scr):
    l = q_ref.shape[1]
    hd = HEAD_DIM
    s_scr[...] = s0_ref[0]

    def per_chunk(ci, carry):
        rows = pl.ds(pl.multiple_of(ci * CHUNK, CHUNK), CHUNK)
        cos = cos_ref[rows, :]
        sin = sin_ref[rows, :]
        q = q_ref[0, rows, :]
        k = k_ref[0, rows, :]
        q = q * cos + _dot3(q, perm_ref[...]) * sin
        k = (k * cos + _dot3(k, perm_ref[...]) * sin) * (hd ** -0.5)
        v = v_ref[0, rows, :]
        for h in range(N_HEADS):
            ls = slice(h * hd, (h + 1) * hd)
            qh, kh, vh = q[:, ls], k[:, ls], v[:, ls].astype(BF16)
            s_h = s_scr[h]
            aqk = _dot(qh.astype(BF16), kh.astype(BF16), 1, 1) * dmask_ref[h]
            o = _dot((qh * eg_ref[h]).astype(BF16), s_h.astype(BF16), 1, 0) + _dot(aqk.astype(BF16), vh, 1, 0)
            s_scr[h] = s_h * dl_ref[h] + _dot((kh * kd_ref[h]).astype(BF16), vh, 0, 0)
            o_scr[rows, ls] = o
        return carry

    lax.fori_loop(0, l // CHUNK, per_chunk, 0)
    sn_ref[0] = s_scr[...]
    for r in range(l // ROW_CHUNK):
        rows = slice(r * ROW_CHUNK, (r + 1) * ROW_CHUNK)
        o = o_scr[rows, :]
        d = o - _dot3(o, bd_ref[...])
        var = _dot3(d * d, bd_ref[...])
        g = g_ref[0, rows, :]
        o_ref[0, rows, :] = g * jax.nn.sigmoid(g) * (d * lax.rsqrt(var + EPS))


def _ret_tables(l, start_pos):
    hd, half = HEAD_DIM, HEAD_DIM // 2
    pos = jnp.arange(l, dtype=F32) + start_pos
    inv = 1.0 / (ROPE_BASE ** (jnp.arange(half, dtype=F32) / half))
    ang = pos[:, None] * inv[None, :]
    cos = jnp.tile(jnp.cos(ang), (1, 2 * N_HEADS))
    sin = jnp.tile(jnp.sin(ang), (1, 2 * N_HEADS))
    lane = np.arange(GROUP_WIDTH)
    perm = np.zeros((GROUP_WIDTH, GROUP_WIDTH), np.float32)
    first = (lane % hd) < half
    perm[lane[first] + half, lane[first]] = -1.0
    perm[lane[~first] - half, lane[~first]] = 1.0
    bd = (lane[:, None] // hd == lane[None, :] // hd).astype(np.float32) / hd
    c = min(CHUNK, l)
    log_gamma = jnp.log1p(-jnp.exp2(-5.0 - jnp.arange(N_HEADS, dtype=F32)))
    G = jnp.cumsum(jnp.broadcast_to(log_gamma[:, None], (N_HEADS, c)), axis=-1)
    idx = jnp.arange(c)
    incl = idx[:, None] >= idx[None, :]
    dmask = jnp.exp(jnp.where(incl, G[:, :, None] - G[:, None, :], -jnp.inf))
    ones = jnp.ones((N_HEADS, c, hd), F32)
    eg = jnp.exp(G)[:, :, None] * ones
    kd = jnp.exp(G[:, -1:] - G)[:, :, None] * ones
    dl = jnp.exp(G[:, -1])[:, None, None] * jnp.ones((N_HEADS, hd, hd), F32)
    return cos, sin, jnp.asarray(perm, BF16), dmask, eg, kd, dl, jnp.asarray(bd, BF16)


def _ret_pallas(z, s0, start_pos):
    b, l, _ = z.shape
    c = GROUP_WIDTH
    tabs = _ret_tables(l, start_pos)
    st_spec = pl.BlockSpec((1, N_HEADS, HEAD_DIM, HEAD_DIM), lambda i: (i, 0, 0, 0))
    return pl.pallas_call(
        _ret_kernel,
        grid=(b,),
        in_specs=[_zspec('rq', l), _zspec('rk', l), _zspec('rv', l), _zspec('rg', l)]
                 + [_full(t.shape) for t in tabs] + [st_spec],
        out_specs=[pl.BlockSpec((1, l, c), lambda i: (i, 0, 0)), st_spec],
        out_shape=[jax.ShapeDtypeStruct((b, l, c), F32), jax.ShapeDtypeStruct(s0.shape, F32)],
        scratch_shapes=[pltpu.VMEM((l, c), F32), pltpu.VMEM((N_HEADS, HEAD_DIM, HEAD_DIM), F32)],
        compiler_params=_cparams("arbitrary"),
        name="retention_mixer",
    )(z, z, z, z, *tabs, s0)


def _split2(x):
    hi = x.astype(BF16)
    return hi, (x - hi.astype(F32)).astype(BF16)


def _dotx(a, b):
    ah, al = _split2(a)
    bh, bl = _split2(b)
    return _dot(ah, bh, 1, 0) + (_dot(ah, bl, 1, 0) + _dot(al, bh, 1, 0))


def _bdot(a, b, ca=1, cb=0):
    return _dot(a.astype(BF16), b.astype(BF16), ca, cb)


def _delta_kernel(qkv_ref, dz_ref, zab_ref, hist_ref, s0_ref, cw_ref, alog_ref, dtb_ref, nw_ref, exp_ref, ones_ref,
                  ltri_ref, utri_ref, o_ref, nh_ref, sn_ref, buf, q_scr, k_scr, v_scr, g_scr, b_scr, o_scr, s_scr):
    l = qkv_ref.shape[1]
    hd, gw = HEAD_DIM, GROUP_WIDTH
    off = 8
    nhist = DELTA_CONV - 1
    buf[0:off - nhist, :] = jnp.zeros((off - nhist, 3 * gw), F32)
    buf[off - nhist:off, :] = hist_ref[0]
    buf[off:off + l, :] = qkv_ref[0]
    for r in range(l // ROW_CHUNK):
        rows = slice(r * ROW_CHUNK, (r + 1) * ROW_CHUNK)
        base = r * ROW_CHUNK + off - nhist
        acc = jnp.zeros((ROW_CHUNK, 3 * gw), F32)
        for w in range(DELTA_CONV):
            acc = acc + buf[base + w:base + w + ROW_CHUNK, :] * cw_ref[w:w + 1, :]
        act = acc * jax.nn.sigmoid(acc)
        q, k = act[:, 0:gw], act[:, gw:2 * gw]
        q_scr[rows, :] = q * lax.rsqrt(_dot3(q * q, ones_ref[...]) + EPS) * (hd ** -0.5)
        k_scr[rows, :] = k * lax.rsqrt(_dot3(k * k, ones_ref[...]) + EPS)
        v_scr[rows, :] = act[:, 2 * gw:3 * gw]
        zab = zab_ref[0, rows, :]
        x = zab + dtb_ref[...]
        softplus = jnp.maximum(x, 0.0) + jnp.log(1.0 + jnp.exp(-jnp.abs(x)))
        g_scr[rows, :] = _dot3(-jnp.exp(alog_ref[...]) * softplus, exp_ref[0])
        b_scr[rows, :] = _dot3(jax.nn.sigmoid(zab), exp_ref[1])
    nh_ref[0] = buf[l + off - nhist:l + off, :]

    s_scr[...] = s0_ref[0]
    ri = lax.broadcasted_iota(jnp.int32, (CHUNK, CHUNK), 0)
    ci_ = lax.broadcasted_iota(jnp.int32, (CHUNK, CHUNK), 1)
    incl = ri >= ci_
    strict = ri > ci_
    eye = jnp.where(ri == ci_, 1.0, 0.0).astype(F32)
    ltri = ltri_ref[...]
    utri = utri_ref[...]

    def per_chunk(ci, carry):
        rows = pl.ds(pl.multiple_of(ci * CHUNK, CHUNK), CHUNK)
        q, k, v = q_scr[rows, :], k_scr[rows, :], v_scr[rows, :]
        gx, bx = g_scr[rows, :], b_scr[rows, :]
        for h in range(N_HEADS):
            ls = slice(h * hd, (h + 1) * hd)
            qh, kh, vh, gh, bh = q[:, ls], k[:, ls], v[:, ls], gx[:, ls], bx[:, ls]
            g3 = _split3(gh)
            gcol = sum(_dot(ltri, p, 1, 0) for p in g3)
            grow = sum(_dot(p, utri, 0, 0) for p in g3)
            dmask = jnp.where(incl, jnp.exp(gcol - grow), 0.0)
            kb = kh * bh
            a = jnp.where(strict, _bdot(kb, kh, 1, 1) * dmask, 0.0)
            t = eye - a
            pw = a
            for _ in range(5):
                pw = _dotx(pw, pw)
                t = t + _dotx(t, pw)
            eg = jnp.exp(gcol)
            u = _bdot(t, vh * bh)
            wm = _bdot(t, kb * eg)
            aqk = _bdot(qh, kh, 1, 1) * dmask
            glast = gcol[CHUNK - 1:CHUNK, :]
            kd = kh * jnp.exp(glast - gcol)
            s_h = s_scr[h]
            v_new = u - _bdot(wm, s_h)
            o_scr[rows, ls] = _bdot(qh * eg, s_h) + _bdot(aqk, v_new)
            s_scr[h] = s_h * jnp.exp(glast) + _bdot(kd, v_new, 0, 0)
        return carry

    lax.fori_loop(0, l // CHUNK, per_chunk, 0)
    sn_ref[0] = s_scr[...]
    for r in range(l // ROW_CHUNK):
        rows = slice(r * ROW_CHUNK, (r + 1) * ROW_CHUNK)
        o = o_scr[rows, :]
        ms = _dot3(o * o, ones_ref[...]) * (1.0 / hd)
        dz = dz_ref[0, rows, :]
        o_ref[0, rows, :] = o * lax.rsqrt(ms + EPS) * nw_ref[...] * (dz * jax.nn.sigmoid(dz))


def _delta_pallas(z, hist, s0, conv_w, a_log, dt_bias, norm_w):
    b, l, _ = z.shape
    c, hd = GROUP_WIDTH, HEAD_DIM
    lane = np.arange(c)
    ones_bd = jnp.asarray(lane[:, None] // hd == lane[None, :] // hd, BF16)
    expand = np.zeros((2, 128, c), np.float32)
    for h in range(N_HEADS):
        expand[0, h, h * hd:(h + 1) * hd] = 1.0
        expand[1, N_HEADS + h, h * hd:(h + 1) * hd] = 1.0
    ltri = jnp.asarray(np.tril(np.ones((CHUNK, CHUNK), np.float32)), BF16)
    pad = lambda v: jnp.zeros((1, 128), F32).at[0, :N_HEADS].set(v)
    st_spec = pl.BlockSpec((1, N_HEADS, hd, hd), lambda i: (i, 0, 0, 0))
    hist_spec = pl.BlockSpec((1, DELTA_CONV - 1, 3 * c), lambda i: (i, 0, 0))
    return pl.pallas_call(
        _delta_kernel,
        grid=(b,),
        in_specs=[pl.BlockSpec((1, l, 3 * c), lambda i: (i, 0, 0)), _zspec('dz', l),
                  pl.BlockSpec((1, l, 128), lambda i: (i, 0, Z_AB // 128)), hist_spec, st_spec,
                  _full((DELTA_CONV, 3 * c)), _full((1, 128)), _full((1, 128)), _full((1, c)),
                  _full((2, 128, c)), _full((c, c)), _full((CHUNK, CHUNK)), _full((CHUNK, CHUNK))],
        out_specs=[pl.BlockSpec((1, l, c), lambda i: (i, 0, 0)), hist_spec, st_spec],
        out_shape=[jax.ShapeDtypeStruct((b, l, c), F32), jax.ShapeDtypeStruct(hist.shape, F32),
                   jax.ShapeDtypeStruct(s0.shape, F32)],
        scratch_shapes=[pltpu.VMEM((l + 8, 3 * c), F32)] + [pltpu.VMEM((l, c), F32)] * 6
                       + [pltpu.VMEM((N_HEADS, hd, hd), F32)],
        compiler_params=_cparams("arbitrary"),
        name="delta_mixer",
    )(z, z, z, hist, s0, conv_w, pad(a_log), pad(dt_bias), jnp.tile(norm_w, N_HEADS).reshape(1, c),
      jnp.asarray(expand, BF16), ones_bd, ltri, ltri.T)


def _pool_blockdiag(pool_w):
    out = jnp.zeros((GROUP_WIDTH, GROUP_WIDTH), pool_w.dtype)
    for gi in range(len(POOL_WINDOWS)):
        sl = slice(gi * POOL_GROUP, (gi + 1) * POOL_GROUP)
        out = out.at[sl, sl].set(pool_w[gi])
    return out


def _permute_w_in(w_in):
    pad = jnp.zeros((w_in.shape[0], Z_WIDTH - Z_AB - 8), w_in.dtype)
    return jnp.concatenate([w_in[:, :OFF_DA], w_in[:, OFF_PU:], w_in[:, OFF_DA:OFF_PU], pad], axis=1)


def _run_trunk(x, states, params, final_norm, start_pos):
    b, l, _ = x.shape
    n = b * l
    xf = x.reshape(n, D_MODEL)
    s_delta, s_dconv, s_pool, s_ret, s_conv = states
    new = ([], [], [], [], [])
    for li in range(DEPTH):
        p = {name: arr[li] for name, arr in params.items()}
        z = _in_proj(xf, p['norm1'].reshape(1, D_MODEL), p['w_in_bf']).reshape(b, l, Z_WIDTH)
        if l % ROW_CHUNK == 0:
            o_a, n_dconv, n_delta = _delta_pallas(z, s_dconv[li], s_delta[li], p['delta_conv_w'], p['delta_a_log'],
                                                  p['delta_dt_bias'], p['delta_norm_w'])
        else:
            o_a, n_dconv, n_delta = _delta_mixer(z, s_dconv[li], s_delta[li], p['delta_conv_w'], p['delta_a_log'],
                                                 p['delta_dt_bias'], p['delta_norm_w'])
        if l % ROW_CHUNK == 0:
            o_b, n_pool = _pool_pallas(z, s_pool[li], p['pool_wbd_bf'], p['pool_scale'], start_pos)
            o_c, n_ret = _ret_pallas(z, s_ret[li], start_pos)
            o_d, n_conv = _conf_pallas(z, s_conv[li], p['conv_dw_w'], p['conv_dw_b'], p['conv_ln_w'],
                                       p['conv_ln_b'], p['conv_pw_bf'])
        else:
            o_b, n_pool = _pool_mixer(z, s_pool[li], start_pos, p['pool_w'], p['pool_scale'])
            o_c, n_ret = _retention_mixer(z, s_ret[li], start_pos)
            o_d, n_conv = _conformer_mixer(z, s_conv[li], p['conv_dw_w'], p['conv_dw_b'], p['conv_ln_w'],
                                           p['conv_ln_b'], p['conv_pw_w'])
        mix = jnp.concatenate([o_a, o_b, o_c, o_d], axis=-1).reshape(n, D_MODEL)
        xf, hn = _out_proj(xf, mix, p['w_out_bf'], p['norm2'].reshape(1, D_MODEL))
        xf = peer_block(hn, xf, p['wqt_bf'], p['keys_bf'], p['u_bf'], p['v_bf'], final_norm.reshape(1, D_MODEL),
                        li == DEPTH - 1)
        for acc, a in zip(new, (n_delta, n_dconv, n_pool, n_ret, n_conv)):
            acc.append(a)
    return xf.reshape(b, l, D_MODEL), tuple(jnp.stack(acc) for acc in new)


def kernel(x_prompt, x_sample, state_delta, state_delta_conv, state_pool, state_ret, state_conv, norm1, w_in, delta_conv_w, delta_a_log, delta_dt_bias, delta_norm_w, pool_w, pool_scale, conv_dw_w, conv_dw_b, conv_ln_w, conv_ln_b, conv_pw_w, w_out, norm2, peer_wq, peer_keys, peer_u, peer_v, final_norm):
    params = {
        'norm1': norm1, 'delta_conv_w': delta_conv_w, 'delta_a_log': delta_a_log, 'delta_dt_bias': delta_dt_bias,
        'delta_norm_w': delta_norm_w, 'pool_w': pool_w, 'pool_scale': pool_scale, 'conv_dw_w': conv_dw_w,
        'conv_dw_b': conv_dw_b, 'conv_ln_w': conv_ln_w, 'conv_ln_b': conv_ln_b, 'conv_pw_w': conv_pw_w,
        'norm2': norm2,
        'w_in_bf': jax.vmap(_permute_w_in)(w_in).astype(BF16),
        'w_out_bf': w_out.astype(BF16),
        'pool_wbd_bf': jax.vmap(_pool_blockdiag)(pool_w).astype(BF16),
        'conv_pw_bf': conv_pw_w.astype(BF16),
        'wqt_bf': jnp.swapaxes(peer_wq, 1, 2).astype(BF16),
        'keys_bf': peer_keys.reshape(DEPTH, 2 * PEER_HEADS * PEER_NKEYS, PEER_DHALF).astype(BF16),
        'u_bf': peer_u.astype(BF16),
        'v_bf': peer_v.astype(BF16),
    }
    bp = x_prompt.shape[0]
    zeros = tuple(jnp.zeros((DEPTH, bp) + s.shape[2:], F32)
                  for s in (state_delta, state_delta_conv, state_pool, state_ret, state_conv))
    y_p, st_p = _run_trunk(x_prompt, zeros, params, final_norm, 0)
    y_s, st_s = _run_trunk(x_sample, (state_delta, state_delta_conv, state_pool, state_ret, state_conv),
                           params, final_norm, PAST_LEN)
    return (y_p, y_s) + st_p + st_s
```

```python
import functools

import numpy as np
import jax
import jax.numpy as jnp
from jax import lax
from jax.experimental import pallas as pl
from jax.experimental.pallas import tpu as pltpu

F32 = jnp.float32
BF16 = jnp.bfloat16

D_MODEL = 1024
DEPTH = 2
HEAD_DIM = 64
GROUP_WIDTH = 256
N_HEADS = 4
POOL_WINDOWS = (2, 4, 8, 16)
POOL_GROUP = 64
POOL_BUF = 15
DELTA_CONV = 4
CONF_CONV = 31
CHUNK = 64
ROPE_BASE = 10000.0
EPS = 1e-6
PAST_LEN = 16384

OFF_DQKV = 0
OFF_DZ = 768
OFF_DA = 1024
OFF_DB = 1028
OFF_PU = 1032
OFF_RQ = 1288
OFF_RK = 1544
OFF_RV = 1800
OFF_RG = 2056
OFF_CG = 2312
IN_WIDTH = 2824
Z_AB = 2816
Z_WIDTH = 2944

PEER_HEADS = 8
PEER_NKEYS = 128
PEER_TOPK = 16
PEER_DHALF = 128
PEER_EXPERTS = PEER_NKEYS * PEER_NKEYS

VMEM_LIMIT = 56 * 1024 * 1024

NEG_INF = float("-inf")
NOT_RANKED = 99.0

PEER_CANDS = tuple((k, l) for k in range(1, PEER_TOPK + 1) for l in range(1, PEER_TOPK + 1) if k * l <= PEER_TOPK)


def _cparams(*sem):
    return pltpu.CompilerParams(dimension_semantics=sem, vmem_limit_bytes=VMEM_LIMIT)


def _dot(a, b, ca, cb, precision=None):
    return lax.dot_general(a, b, (((ca,), (cb,)), ((), ())), preferred_element_type=F32, precision=precision)


def _in_proj_kernel(x_ref, nw_ref, w_ref, z_ref):
    x = x_ref[...]
    h = x * lax.rsqrt(jnp.mean(x * x, axis=-1, keepdims=True) + EPS) * nw_ref[...]
    z_ref[...] = _dot(h.astype(BF16), w_ref[...], 1, 0)


def _in_proj(x, nw, w_bf):
    n = x.shape[0]
    tm = min(256, n)
    return pl.pallas_call(
        _in_proj_kernel,
        grid=(n // tm,),
        in_specs=[pl.BlockSpec((tm, D_MODEL), lambda i: (i, 0)),
                  pl.BlockSpec((1, D_MODEL), lambda i: (0, 0)),
                  pl.BlockSpec(w_bf.shape, lambda i: (0, 0))],
        out_specs=pl.BlockSpec((tm, w_bf.shape[1]), lambda i: (i, 0)),
        out_shape=jax.ShapeDtypeStruct((n, w_bf.shape[1]), F32),
        compiler_params=_cparams("arbitrary"),
        name="in_proj",
    )(x, nw, w_bf)


def _out_proj_kernel(x_ref, mix_ref, w_ref, nw_ref, xo_ref, hn_ref):
    y = x_ref[...] + _dot(mix_ref[...].astype(BF16), w_ref[...], 1, 0)
    xo_ref[...] = y
    hn = y * lax.rsqrt(jnp.mean(y * y, axis=-1, keepdims=True) + EPS) * nw_ref[...]
    hn_ref[...] = hn.astype(BF16)


def _out_proj(x, mix, w_bf, nw):
    n = x.shape[0]
    tm = min(256, n)
    tok = pl.BlockSpec((tm, D_MODEL), lambda i: (i, 0))
    return pl.pallas_call(
        _out_proj_kernel,
        grid=(n // tm,),
        in_specs=[tok, tok, pl.BlockSpec(w_bf.shape, lambda i: (0, 0)), pl.BlockSpec((1, D_MODEL), lambda i: (0, 0))],
        out_specs=[tok, tok],
        out_shape=[jax.ShapeDtypeStruct((n, D_MODEL), F32), jax.ShapeDtypeStruct((n, D_MODEL), BF16)],
        compiler_params=_cparams("arbitrary"),
        name="out_proj",
    )(x, mix, w_bf, nw)


def _topk_ranks(s):
    nk, t = s.shape
    iota = lax.broadcasted_iota(jnp.int32, (nk, t), 0).astype(F32)
    kiota = lax.broadcasted_iota(jnp.int32, (PEER_TOPK, t), 0)

    def extract(k, carry):
        s, r, tops = carry
        m = jnp.max(s, axis=0, keepdims=True)
        idx = jnp.min(jnp.where(s == m, iota, float(nk)), axis=0, keepdims=True)
        one = iota == idx
        tops = jnp.where(kiota == k, m, tops)
        return jnp.where(one, NEG_INF, s), jnp.where(one, lax.convert_element_type(k + 1, F32), r), tops

    init = (s, jnp.full((nk, t), NOT_RANKED, F32), jnp.zeros((PEER_TOPK, t), F32))
    _, r, tops = lax.fori_loop(0, PEER_TOPK, extract, init)
    return tops, r


def _select_pairs(a, b):
    cand = [a[k - 1] + b[l - 1] for (k, l) in PEER_CANDS]
    sel = [jnp.zeros_like(cand[0]) for _ in PEER_CANDS]
    zsum = jnp.zeros_like(cand[0])
    m0 = functools.reduce(jnp.maximum, cand)
    for _ in range(PEER_TOPK):
        m = functools.reduce(jnp.maximum, cand)
        zsum = zsum + jnp.exp(m - m0)
        found = jnp.zeros_like(m)
        for ci in range(len(PEER_CANDS)):
            hit = jnp.where(cand[ci] == m, 1.0 - found, 0.0)
            found = found + hit
            sel[ci] = sel[ci] + hit
            cand[ci] = jnp.where(hit > 0.0, NEG_INF, cand[ci])
    height = []
    for k in range(1, PEER_TOPK + 1):
        hk = jnp.zeros_like(zsum)
        for ci, (kk, _) in enumerate(PEER_CANDS):
            if kk == k:
                hk = hk + sel[ci]
        height.append(hk)
    return height, 1.0 / zsum


def _peer_a_kernel(hn_ref, wqt_ref, keys_ref, e0_ref, lrow_ref, r1_ref, e1_ref, q_scr, s_scr, top_scr):
    ta = hn_ref.shape[0]
    nk = PEER_NKEYS
    q_scr[...] = _dot(wqt_ref[...], hn_ref[...], 1, 1).astype(BF16)

    for hp in range(2 * PEER_HEADS):
        h, p = hp // 2, hp % 2
        hp_rows = slice(hp * nk, (hp + 1) * nk)
        s_all = _dot(keys_ref[hp_rows, :], q_scr[hp_rows, :], 1, 0)
        s_scr[hp_rows, :] = s_all
        for lt in range(ta // 128):
            lanes = slice(lt * 128, (lt + 1) * 128)
            tops, r = _topk_ranks(s_all[:, lanes])
            for k in range(PEER_TOPK):
                top_scr[p, k, h:h + 1, lanes] = tops[k:k + 1, :]
            (lrow_ref if p == 0 else r1_ref)[h * nk:(h + 1) * nk, lanes] = r

    a = [top_scr[0, k] for k in range(PEER_TOPK)]
    b = [top_scr[1, k] for k in range(PEER_TOPK)]
    height, inv_z = _select_pairs(a, b)
    for h in range(PEER_HEADS):
        rows = slice(h * nk, (h + 1) * nk)
        r0 = lrow_ref[rows, :]
        lrow = jnp.zeros_like(r0)
        for k in range(1, PEER_TOPK + 1):
            lrow = jnp.where(r0 == float(k), height[k - 1][h:h + 1, :], lrow)
        lrow_ref[rows, :] = lrow
        s0 = s_scr[2 * h * nk:(2 * h + 1) * nk, :]
        s1 = s_scr[(2 * h + 1) * nk:(2 * h + 2) * nk, :]
        e0_ref[rows, :] = jnp.exp(s0 - a[0][h:h + 1, :]) * inv_z[h:h + 1, :]
        e1_ref[rows, :] = jnp.exp(s1 - b[0][h:h + 1, :])


def _peer_a(hn, wqt, keys2, ta):
    n = hn.shape[0]
    rows = PEER_HEADS * PEER_NKEYS
    out = jax.ShapeDtypeStruct((rows, n), F32)
    fac_spec = pl.BlockSpec((rows, ta), lambda t: (0, t))
    return pl.pallas_call(
        _peer_a_kernel,
        grid=(n // ta,),
        in_specs=[
            pl.BlockSpec((ta, D_MODEL), lambda t: (t, 0)),
            pl.BlockSpec(wqt.shape, lambda t: (0, 0)),
            pl.BlockSpec(keys2.shape, lambda t: (0, 0)),
        ],
        out_specs=[fac_spec] * 4,
        out_shape=[out] * 4,
        scratch_shapes=[
            pltpu.VMEM((2 * rows, ta), BF16),
            pltpu.VMEM((2 * rows, ta), F32),
            pltpu.VMEM((2, PEER_TOPK, PEER_HEADS, ta), F32),
        ],
        compiler_params=_cparams("arbitrary"),
        name="peer_topk",
    )(hn, wqt, keys2)


def _gelu_tanh(x):
    return 0.5 * x * (1.0 + jnp.tanh(0.7978845608028654 * (x + 0.044715 * (x * x * x))))


def _peer_b_kernel(hn_ref, x_ref, e0_ref, lrow_ref, r1_ref, e1_ref, u_ref, v_ref, fw_ref, o_ref, acc_ref, w_scr,
                   *, rows_per_chunk, final_norm):
    c = pl.program_id(1)
    nk = PEER_NKEYS
    t = hn_ref.shape[0]

    @pl.when(c == 0)
    def _():
        acc_ref[...] = jnp.zeros_like(acc_ref)

    w_scr[...] = _gelu_tanh(_dot(u_ref[...], hn_ref[...], 1, 1))

    def per_row(ii, carry):
        i = c * rows_per_chunk + ii
        lrow = [jnp.broadcast_to(lrow_ref[pl.ds(h * nk + i, 1), :], (8, t)) for h in range(PEER_HEADS)]
        e0 = [jnp.broadcast_to(e0_ref[pl.ds(h * nk + i, 1), :], (8, t)) for h in range(PEER_HEADS)]
        for jt in range(nk // 8):
            g = jnp.zeros((8, t), F32)
            for h in range(PEER_HEADS):
                rows = slice(h * nk + jt * 8, h * nk + jt * 8 + 8)
                g = g + jnp.where(r1_ref[rows, :] <= lrow[h], e1_ref[rows, :], 0.0) * e0[h]
            wrow = pl.ds(pl.multiple_of(ii * nk + jt * 8, 8), 8)
            w_scr[wrow, :] = w_scr[wrow, :] * g
        return carry

    lax.fori_loop(0, rows_per_chunk, per_row, 0)
    acc_ref[...] += _dot(w_scr[...].astype(BF16), v_ref[...], 0, 0)

    @pl.when(c == pl.num_programs(1) - 1)
    def _():
        y = x_ref[...] + acc_ref[...]
        if final_norm:
            y = y * lax.rsqrt(jnp.mean(y * y, axis=-1, keepdims=True) + EPS) * fw_ref[...]
        o_ref[...] = y


def _peer_b(hn, x, facs, u_bf, v_bf, final_w, t, rows_per_chunk, final_norm):
    n = hn.shape[0]
    rows = PEER_HEADS * PEER_NKEYS
    ec = rows_per_chunk * PEER_NKEYS
    fac_spec = pl.BlockSpec((rows, t), lambda ti, c: (0, ti))
    tok_spec = pl.BlockSpec((t, D_MODEL), lambda ti, c: (ti, 0))
    exp_spec = pl.BlockSpec((ec, D_MODEL), lambda ti, c: (c, 0))
    return pl.pallas_call(
        functools.partial(_peer_b_kernel, rows_per_chunk=rows_per_chunk, final_norm=final_norm),
        grid=(n // t, PEER_EXPERTS // ec),
        in_specs=[tok_spec, tok_spec, fac_spec, fac_spec, fac_spec, fac_spec, exp_spec, exp_spec,
                  pl.BlockSpec((1, D_MODEL), lambda ti, c: (0, 0))],
        out_specs=tok_spec,
        out_shape=jax.ShapeDtypeStruct((n, D_MODEL), F32),
        scratch_shapes=[pltpu.VMEM((t, D_MODEL), F32), pltpu.VMEM((ec, t), F32)],
        compiler_params=_cparams("arbitrary", "arbitrary"),
        name="peer_experts",
    )(hn, x, *facs, u_bf, v_bf, final_w)


def peer_block(hn, x, wqt, keys2, u_bf, v_bf, final_w, final_norm):
    n = hn.shape[0]
    facs = _peer_a(hn, wqt, keys2, min(256, n))
    return _peer_b(hn, x, facs, u_bf, v_bf, final_w, min(512, n), 8, final_norm)


def _causal_dwconv(u, buf, w):
    full = jnp.concatenate([buf.astype(u.dtype), u], axis=1)
    y = lax.conv_general_dilated(full, w[:, None, :].astype(u.dtype), window_strides=(1,), padding='VALID',
                                 dimension_numbers=('NWC', 'WIO', 'NWC'), feature_group_count=u.shape[-1])
    return y, full[:, full.shape[1] - (w.shape[0] - 1):]


def _chunk(x, c):
    b, l = x.shape[:2]
    n = -(-l // c)
    x = jnp.pad(x, [(0, 0), (0, n * c - l)] + [(0, 0)] * (x.ndim - 2))
    x = x.reshape((b, n, c) + x.shape[2:])
    return jnp.transpose(x, (1, 0, 3, 2) + tuple(range(4, x.ndim)))


def _unchunk(o, l):
    n, b, h, c, d = o.shape
    return jnp.transpose(o, (1, 0, 3, 2, 4)).reshape(b, n * c, h, d)[:, :l]


def _decay_masks(g, c):
    G = jnp.cumsum(g, axis=-1)
    idx = jnp.arange(c)
    incl = idx[:, None] >= idx[None, :]
    diff = G[..., :, None] - G[..., None, :]
    dmask = jnp.exp(jnp.where(incl, diff, -jnp.inf))
    return G, dmask, idx


def _gated_delta_rule(q, k, v, beta, g, S0):
    l = q.shape[1]
    c = min(CHUNK, l)
    q, k, v, beta, g = (_chunk(t, c) for t in (q, k, v, beta, g))
    G, dmask, idx = _decay_masks(g, c)
    strict = idx[:, None] > idx[None, :]
    kb = k * beta[..., None]
    A = jnp.where(strict, jnp.einsum('...id,...jd->...ij', kb, k) * dmask, 0.0)
    eye = jnp.eye(c, dtype=A.dtype)
    T = lax.linalg.triangular_solve(eye + A, jnp.broadcast_to(eye, A.shape), left_side=True, lower=True)
    eG = jnp.exp(G)[..., None]
    U = jnp.einsum('...ij,...jd->...id', T, v * beta[..., None])
    W = jnp.einsum('...ij,...jd->...id', T, kb * eG)
    Qd = q * eG
    Aqk = jnp.einsum('...id,...jd->...ij', q, k) * dmask
    Glast = G[..., -1:]
    Kd = k * jnp.exp(Glast - G)[..., None]
    dlast = jnp.exp(Glast[..., 0])

    def step(S, xs):
        U_i, W_i, Qd_i, A_i, Kd_i, dl_i = xs
        v_new = U_i - jnp.einsum('bhcd,bhde->bhce', W_i, S)
        o = jnp.einsum('bhcd,bhde->bhce', Qd_i, S) + jnp.einsum('bhij,bhje->bhie', A_i, v_new)
        S = S * dl_i[..., None, None] + jnp.einsum('bhcd,bhce->bhde', Kd_i, v_new)
        return S, o

    S, o = lax.scan(step, S0, (U, W, Qd, Aqk, Kd, dlast))
    return _unchunk(o, l), S


def _decayed_linear_attn(q, k, v, g, S0):
    l = q.shape[1]
    c = min(CHUNK, l)
    q, k, v, g = (_chunk(t, c) for t in (q, k, v, g))
    G, dmask, _ = _decay_masks(g, c)
    Aqk = jnp.einsum('...id,...jd->...ij', q, k) * dmask
    Qd = q * jnp.exp(G)[..., None]
    Glast = G[..., -1:]
    Kd = k * jnp.exp(Glast - G)[..., None]
    dlast = jnp.exp(Glast[..., 0])

    def step(S, xs):
        A_i, V_i, Qd_i, Kd_i, dl_i = xs
        o = jnp.einsum('bhcd,bhde->bhce', Qd_i, S) + jnp.einsum('bhij,bhje->bhie', A_i, V_i)
        S = S * dl_i[..., None, None] + jnp.einsum('bhcd,bhce->bhde', Kd_i, V_i)
        return S, o

    S, o = lax.scan(step, S0, (Aqk, v, Qd, Kd, dlast))
    return _unchunk(o, l), S


def _rotary(x, pos):
    half = x.shape[-1] // 2
    inv = 1.0 / (ROPE_BASE ** (jnp.arange(half, dtype=F32) / half))
    ang = pos[:, None] * inv[None, :]
    cos = jnp.cos(ang)[None, :, None, :]
    sin = jnp.sin(ang)[None, :, None, :]
    x1, x2 = x[..., :half], x[..., half:]
    return jnp.concatenate([x1 * cos - x2 * sin, x1 * sin + x2 * cos], axis=-1)


def _l2norm(x):
    return x * lax.rsqrt(jnp.sum(x * x, axis=-1, keepdims=True) + EPS)


def _delta_mixer(z, s_dconv, s_delta, conv_w, a_log, dt_bias, norm_w):
    b, l, _ = z.shape
    H, dh = N_HEADS, HEAD_DIM
    qkv, new_buf = _causal_dwconv(z[..., 0:768], s_dconv, conv_w)
    qkv = jax.nn.silu(qkv)
    q = _l2norm(qkv[..., :GROUP_WIDTH].reshape(b, l, H, dh)) * (dh ** -0.5)
    k = _l2norm(qkv[..., GROUP_WIDTH:2 * GROUP_WIDTH].reshape(b, l, H, dh))
    v = qkv[..., 2 * GROUP_WIDTH:].reshape(b, l, H, dh)
    a = z[..., Z_AB:Z_AB + H]
    beta = jax.nn.sigmoid(z[..., Z_AB + H:Z_AB + 2 * H])
    g = -jnp.exp(a_log) * jax.nn.softplus(a + dt_bias)
    o, S = _gated_delta_rule(q, k, v, beta, g, s_delta)
    o = o * lax.rsqrt(jnp.mean(o * o, axis=-1, keepdims=True) + EPS) * norm_w
    o = o.reshape(b, l, GROUP_WIDTH) * jax.nn.silu(z[..., 768:1024])
    return o, new_buf, S


def _pool_mixer(z, s_pool, start_pos, pool_w, pool_scale):
    u = z[..., 1024:1280]
    b, l, _ = u.shape
    full = jnp.concatenate([s_pool, u], axis=1)
    new_buf = full[:, l:]
    cs = jnp.concatenate([jnp.zeros((b, 1, GROUP_WIDTH), F32), jnp.cumsum(full, axis=1)], axis=1)
    t = jnp.arange(l)
    means = []
    for gi, w in enumerate(POOL_WINDOWS):
        sl = slice(gi * POOL_GROUP, (gi + 1) * POOL_GROUP)
        wsum = cs[:, POOL_BUF + 1:POOL_BUF + 1 + l, sl] - cs[:, POOL_BUF + 1 - w:POOL_BUF + 1 - w + l, sl]
        cnt = jnp.minimum(w, start_pos + t + 1).astype(F32)
        means.append(wsum / cnt[None, :, None])
    p = (jnp.concatenate(means, axis=-1) - full[:, POOL_BUF:]).reshape(b, l, len(POOL_WINDOWS), POOL_GROUP)
    y = jnp.einsum('blgc,gcd->blgd', p, pool_w).reshape(b, l, GROUP_WIDTH)
    return y * pool_scale, new_buf


def _retention_mixer(z, s_ret, start_pos):
    b, l, _ = z.shape
    H, dh = N_HEADS, HEAD_DIM
    q = z[..., 1280:1536].reshape(b, l, H, dh)
    k = z[..., 1536:1792].reshape(b, l, H, dh)
    v = z[..., 1792:2048].reshape(b, l, H, dh)
    gate = z[..., 2048:2304]
    pos = jnp.arange(l, dtype=F32) + start_pos
    q = _rotary(q, pos)
    k = _rotary(k, pos) * (dh ** -0.5)
    log_gamma = jnp.log1p(-jnp.exp2(-5.0 - jnp.arange(H, dtype=F32)))
    g = jnp.broadcast_to(log_gamma, (b, l, H))
    o, S = _decayed_linear_attn(q, k, v, g, s_ret)
    mu = jnp.mean(o, axis=-1, keepdims=True)
    var = jnp.mean(jnp.square(o - mu), axis=-1, keepdims=True)
    o = (o - mu) * lax.rsqrt(var + EPS)
    return jax.nn.silu(gate) * o.reshape(b, l, GROUP_WIDTH), S


def _conformer_mixer(z, s_conv, dw_w, dw_b, ln_w, ln_b, pw_w):
    a = z[..., 2304:2560]
    gt = z[..., 2560:2816]
    glu = a * jax.nn.sigmoid(gt)
    dc, new_buf = _causal_dwconv(glu, s_conv, dw_w)
    dc = dc + dw_b
    mu = jnp.mean(dc, axis=-1, keepdims=True)
    var = jnp.mean(jnp.square(dc - mu), axis=-1, keepdims=True)
    hn = (dc - mu) * lax.rsqrt(var + EPS) * ln_w + ln_b
    return jnp.einsum('blc,cd->bld', jax.nn.silu(hn), pw_w), new_buf


ROW_CHUNK = 256
Z_COL = {'dz': 3, 'pu': 4, 'rq': 5, 'rk': 6, 'rv': 7, 'rg': 8, 'ca': 9, 'cg': 10}


def _zspec(name, l):
    return pl.BlockSpec((1, l, GROUP_WIDTH), lambda b, c=Z_COL[name]: (b, 0, c))


def _full(shape):
    return pl.BlockSpec(shape, lambda b: (0,) * len(shape))


def _split3(x):
    hi = x.astype(BF16)
    r = x - hi.astype(F32)
    mid = r.astype(BF16)
    lo = (r - mid.astype(F32)).astype(BF16)
    return hi, mid, lo


def _dot3(x, m_bf):
    hi, mid, lo = _split3(x)
    return _dot(hi, m_bf, 1, 0) + _dot(mid, m_bf, 1, 0) + _dot(lo, m_bf, 1, 0)


def _conf_kernel(a_ref, gt_ref, hist_ref, dw_ref, db_ref, lnw_ref, lnb_ref, pw_ref, o_ref, nh_ref, buf):
    l = a_ref.shape[1]
    off = 32
    buf[0:off - (CONF_CONV - 1), :] = jnp.zeros((off - (CONF_CONV - 1), GROUP_WIDTH), F32)
    buf[off - (CONF_CONV - 1):off, :] = hist_ref[0]
    buf[off:off + l, :] = a_ref[0] * jax.nn.sigmoid(gt_ref[0])
    for r in range(l // ROW_CHUNK):
        base = r * ROW_CHUNK + off - (CONF_CONV - 1)
        acc = jnp.zeros((ROW_CHUNK, GROUP_WIDTH), F32) + db_ref[...]
        for w in range(CONF_CONV):
            acc = acc + buf[base + w:base + w + ROW_CHUNK, :] * dw_ref[w:w + 1, :]
        mu = jnp.mean(acc, axis=-1, keepdims=True)
        d = acc - mu
        var = jnp.mean(d * d, axis=-1, keepdims=True)
        hn = d * lax.rsqrt(var + EPS) * lnw_ref[...] + lnb_ref[...]
        act = hn * jax.nn.sigmoid(hn)
        o_ref[0, r * ROW_CHUNK:(r + 1) * ROW_CHUNK, :] = _dot(act.astype(BF16), pw_ref[...], 1, 0)
    nh_ref[0] = buf[l + off - (CONF_CONV - 1):l + off, :]


def _conf_pallas(z, hist, dw, db, lnw, lnb, pw_bf):
    b, l, _ = z.shape
    c = GROUP_WIDTH
    row = lambda v: v.reshape(1, c)
    return pl.pallas_call(
        _conf_kernel,
        grid=(b,),
        in_specs=[_zspec('ca', l), _zspec('cg', l), pl.BlockSpec((1, CONF_CONV - 1, c), lambda i: (i, 0, 0)),
                  _full((CONF_CONV, c)), _full((1, c)), _full((1, c)), _full((1, c)), _full((c, c))],
        out_specs=[pl.BlockSpec((1, l, c), lambda i: (i, 0, 0)), pl.BlockSpec((1, CONF_CONV - 1, c), lambda i: (i, 0, 0))],
        out_shape=[jax.ShapeDtypeStruct((b, l, c), F32), jax.ShapeDtypeStruct((b, CONF_CONV - 1, c), F32)],
        scratch_shapes=[pltpu.VMEM((l + 32, c), F32)],
        compiler_params=_cparams("arbitrary"),
        name="conformer_mixer",
    )(z, z, hist, dw, row(db), row(lnw), row(lnb), pw_bf)


def _pool_kernel(u_ref, hist_ref, wbd_ref, scale_ref, o_ref, nh_ref, buf, *, start_pos):
    l = u_ref.shape[1]
    off = 16
    buf[0:off - POOL_BUF, :] = jnp.zeros((off - POOL_BUF, GROUP_WIDTH), F32)
    buf[off - POOL_BUF:off, :] = hist_ref[0]
    buf[off:off + l, :] = u_ref[0]
    lane = lax.broadcasted_iota(jnp.int32, (ROW_CHUNK, GROUP_WIDTH), 1)
    rowi = lax.broadcasted_iota(jnp.int32, (ROW_CHUNK, GROUP_WIDTH), 0)
    for r in range(l // ROW_CHUNK):
        base = r * ROW_CHUNK + off
        avail = (rowi + (r * ROW_CHUNK + start_pos + 1)).astype(F32)
        cur = buf[base:base + ROW_CHUNK, :]
        s = cur
        mean = None
        for back in range(1, max(POOL_WINDOWS)):
            s = s + buf[base - back:base - back + ROW_CHUNK, :]
            if back + 1 in POOL_WINDOWS:
                gi = POOL_WINDOWS.index(back + 1)
                m = s / jnp.minimum(float(back + 1), avail)
                mean = m if mean is None else jnp.where(lane >= gi * POOL_GROUP, m, mean)
        p = mean - cur
        y = _dot(p.astype(BF16), wbd_ref[...], 1, 0) * scale_ref[...]
        o_ref[0, r * ROW_CHUNK:(r + 1) * ROW_CHUNK, :] = y
    nh_ref[0] = buf[l + off - POOL_BUF:l + off, :]


def _pool_pallas(z, hist, wbd_bf, scale, start_pos):
    b, l, _ = z.shape
    c = GROUP_WIDTH
    return pl.pallas_call(
        functools.partial(_pool_kernel, start_pos=start_pos),
        grid=(b,),
        in_specs=[_zspec('pu', l), pl.BlockSpec((1, POOL_BUF, c), lambda i: (i, 0, 0)), _full((c, c)), _full((1, c))],
        out_specs=[pl.BlockSpec((1, l, c), lambda i: (i, 0, 0)), pl.BlockSpec((1, POOL_BUF, c), lambda i: (i, 0, 0))],
        out_shape=[jax.ShapeDtypeStruct((b, l, c), F32), jax.ShapeDtypeStruct((b, POOL_BUF, c), F32)],
        scratch_shapes=[pltpu.VMEM((l + 16, c), F32)],
        compiler_params=_cparams("arbitrary"),
        name="pool_mixer",
    )(z, hist, wbd_bf, scale.reshape(1, c))


def _ret_kernel(q_ref, k_ref, v_ref, g_ref, cos_ref, sin_ref, perm_ref, dmask_ref, eg_ref, kd_ref, dl_ref, bd_ref,
                s0_ref, o_ref, sn_ref, o_scr, s_scr):
    l = q_ref.shape[1]
    hd = HEAD_DIM
    s_scr[...] = s0_ref[0]

    def per_chunk(ci, carry):
        rows = pl.ds(pl.multiple_of(ci * CHUNK, CHUNK), CHUNK)
        cos = cos_ref[rows, :]
        sin = sin_ref[rows, :]
        q = q_ref[0, rows, :]
        k = k_ref[0, rows, :]
        q = q * cos + _dot3(q, perm_ref[...]) * sin
        k = (k * cos + _dot3(k, perm_ref[...]) * sin) * (hd ** -0.5)
        v = v_ref[0, rows, :]
        for h in range(N_HEADS):
            ls = slice(h * hd, (h + 1) * hd)
            qh, kh, vh = q[:, ls], k[:, ls], v[:, ls].astype(BF16)
            s_h = s_scr[h]
            aqk = _dot(qh.astype(BF16), kh.astype(BF16), 1, 1) * dmask_ref[h]
            o = _dot((qh * eg_ref[h]).astype(BF16), s_h.astype(BF16), 1, 0) + _dot(aqk.astype(BF16), vh, 1, 0)
            s_scr[h] = s_h * dl_ref[h] + _dot((kh * kd_ref[h]).astype(BF16), vh, 0, 0)
            o_scr[rows, ls] = o
        return carry

    lax.fori_loop(0, l // CHUNK, per_chunk, 0)
    sn_ref[0] = s_scr[...]
    for r in range(l // ROW_CHUNK):
        rows = slice(r * ROW_CHUNK, (r + 1) * ROW_CHUNK)
        o = o_scr[rows, :]
        d = o - _dot3(o, bd_ref[...])
        var = _dot3(d * d, bd_ref[...])
        g = g_ref[0, rows, :]
        o_ref[0, rows, :] = g * jax.nn.sigmoid(g) * (d * lax.rsqrt(var + EPS))


def _ret_tables(l, start_pos):
    hd, half = HEAD_DIM, HEAD_DIM // 2
    pos = jnp.arange(l, dtype=F32) + start_pos
    inv = 1.0 / (ROPE_BASE ** (jnp.arange(half, dtype=F32) / half))
    ang = pos[:, None] * inv[None, :]
    cos = jnp.tile(jnp.cos(ang), (1, 2 * N_HEADS))
    sin = jnp.tile(jnp.sin(ang), (1, 2 * N_HEADS))
    lane = np.arange(GROUP_WIDTH)
    perm = np.zeros((GROUP_WIDTH, GROUP_WIDTH), np.float32)
    first = (lane % hd) < half
    perm[lane[first] + half, lane[first]] = -1.0
    perm[lane[~first] - half, lane[~first]] = 1.0
    bd = (lane[:, None] // hd == lane[None, :] // hd).astype(np.float32) / hd
    c = min(CHUNK, l)
    log_gamma = jnp.log1p(-jnp.exp2(-5.0 - jnp.arange(N_HEADS, dtype=F32)))
    G = jnp.cumsum(jnp.broadcast_to(log_gamma[:, None], (N_HEADS, c)), axis=-1)
    idx = jnp.arange(c)
    incl = idx[:, None] >= idx[None, :]
    dmask = jnp.exp(jnp.where(incl, G[:, :, None] - G[:, None, :], -jnp.inf))
    ones = jnp.ones((N_HEADS, c, hd), F32)
    eg = jnp.exp(G)[:, :, None] * ones
    kd = jnp.exp(G[:, -1:] - G)[:, :, None] * ones
    dl = jnp.exp(G[:, -1])[:, None, None] * jnp.ones((N_HEADS, hd, hd), F32)
    return cos, sin, jnp.asarray(perm, BF16), dmask, eg, kd, dl, jnp.asarray(bd, BF16)


def _ret_pallas(z, s0, start_pos):
    b, l, _ = z.shape
    c = GROUP_WIDTH
    tabs = _ret_tables(l, start_pos)
    st_spec = pl.BlockSpec((1, N_HEADS, HEAD_DIM, HEAD_DIM), lambda i: (i, 0, 0, 0))
    return pl.pallas_call(
        _ret_kernel,
        grid=(b,),
        in_specs=[_zspec('rq', l), _zspec('rk', l), _zspec('rv', l), _zspec('rg', l)]
                 + [_full(t.shape) for t in tabs] + [st_spec],
        out_specs=[pl.BlockSpec((1, l, c), lambda i: (i, 0, 0)), st_spec],
        out_shape=[jax.ShapeDtypeStruct((b, l, c), F32), jax.ShapeDtypeStruct(s0.shape, F32)],
        scratch_shapes=[pltpu.VMEM((l, c), F32), pltpu.VMEM((N_HEADS, HEAD_DIM, HEAD_DIM), F32)],
        compiler_params=_cparams("arbitrary"),
        name="retention_mixer",
    )(z, z, z, z, *tabs, s0)


def _split2(x):
    hi = x.astype(BF16)
    return hi, (x - hi.astype(F32)).astype(BF16)


def _dotx(a, b):
    ah, al = _split2(a)
    bh, bl = _split2(b)
    return _dot(ah, bh, 1, 0) + (_dot(ah, bl, 1, 0) + _dot(al, bh, 1, 0))


def _bdot(a, b, ca=1, cb=0):
    return _dot(a.astype(BF16), b.astype(BF16), ca, cb)


def _delta_kernel(qkv_ref, dz_ref, zab_ref, hist_ref, s0_ref, cw_ref, alog_ref, dtb_ref, nw_ref, exp_ref, ones_ref,
                  ltri_ref, utri_ref, o_ref, nh_ref, sn_ref, buf, q_scr, k_scr, v_scr, g_scr, b_scr, o_scr, s_scr):
    l = qkv_ref.shape[1]
    hd, gw = HEAD_DIM, GROUP_WIDTH
    off = 8
    nhist = DELTA_CONV - 1
    buf[0:off - nhist, :] = jnp.zeros((off - nhist, 3 * gw), F32)
    buf[off - nhist:off, :] = hist_ref[0]
    buf[off:off + l, :] = qkv_ref[0]
    for r in range(l // ROW_CHUNK):
        rows = slice(r * ROW_CHUNK, (r + 1) * ROW_CHUNK)
        base = r * ROW_CHUNK + off - nhist
        acc = jnp.zeros((ROW_CHUNK, 3 * gw), F32)
        for w in range(DELTA_CONV):
            acc = acc + buf[base + w:base + w + ROW_CHUNK, :] * cw_ref[w:w + 1, :]
        act = acc * jax.nn.sigmoid(acc)
        q, k = act[:, 0:gw], act[:, gw:2 * gw]
        q_scr[rows, :] = q * lax.rsqrt(_dot3(q * q, ones_ref[...]) + EPS) * (hd ** -0.5)
        k_scr[rows, :] = k * lax.rsqrt(_dot3(k * k, ones_ref[...]) + EPS)
        v_scr[rows, :] = act[:, 2 * gw:3 * gw]
        zab = zab_ref[0, rows, :]
        x = zab + dtb_ref[...]
        softplus = jnp.maximum(x, 0.0) + jnp.log(1.0 + jnp.exp(-jnp.abs(x)))
        g_scr[rows, :] = _dot3(-jnp.exp(alog_ref[...]) * softplus, exp_ref[0])
        b_scr[rows, :] = _dot3(jax.nn.sigmoid(zab), exp_ref[1])
    nh_ref[0] = buf[l + off - nhist:l + off, :]

    s_scr[...] = s0_ref[0]
    ri = lax.broadcasted_iota(jnp.int32, (CHUNK, CHUNK), 0)
    ci_ = lax.broadcasted_iota(jnp.int32, (CHUNK, CHUNK), 1)
    incl = ri >= ci_
    strict = ri > ci_
    eye = jnp.where(ri == ci_, 1.0, 0.0).astype(F32)
    ltri = ltri_ref[...]
    utri = utri_ref[...]

    def per_chunk(ci, carry):
        rows = pl.ds(pl.multiple_of(ci * CHUNK, CHUNK), CHUNK)
        q, k, v = q_scr[rows, :], k_scr[rows, :], v_scr[rows, :]
        gx, bx = g_scr[rows, :], b_scr[rows, :]
        for h in range(N_HEADS):
            ls = slice(h * hd, (h + 1) * hd)
            qh, kh, vh, gh, bh = q[:, ls], k[:, ls], v[:, ls], gx[:, ls], bx[:, ls]
            g3 = _split3(gh)
            gcol = sum(_dot(ltri, p, 1, 0) for p in g3)
            grow = sum(_dot(p, utri, 0, 0) for p in g3)
            dmask = jnp.where(incl, jnp.exp(gcol - grow), 0.0)
            kb = kh * bh
            a = jnp.where(strict, _bdot(kb, kh, 1, 1) * dmask, 0.0)
            t = eye - a
            pw = a
            for _ in range(5):
                pw = _dotx(pw, pw)
                t = t + _dotx(t, pw)
            eg = jnp.exp(gcol)
            u = _bdot(t, vh * bh)
            wm = _bdot(t, kb * eg)
            aqk = _bdot(qh, kh, 1, 1) * dmask
            glast = gcol[CHUNK - 1:CHUNK, :]
            kd = kh * jnp.exp(glast - gcol)
            s_h = s_scr[h]
            v_new = u - _bdot(wm, s_h)
            o_scr[rows, ls] = _bdot(qh * eg, s_h) + _bdot(aqk, v_new)
            s_scr[h] = s_h * jnp.exp(glast) + _bdot(kd, v_new, 0, 0)
        return carry

    lax.fori_loop(0, l // CHUNK, per_chunk, 0)
    sn_ref[0] = s_scr[...]
    for r in range(l // ROW_CHUNK):
        rows = slice(r * ROW_CHUNK, (r + 1) * ROW_CHUNK)
        o = o_scr[rows, :]
        ms = _dot3(o * o, ones_ref[...]) * (1.0 / hd)
        dz = dz_ref[0, rows, :]
        o_ref[0, rows, :] = o * lax.rsqrt(ms + EPS) * nw_ref[...] * (dz * jax.nn.sigmoid(dz))


def _delta_pallas(z, hist, s0, conv_w, a_log, dt_bias, norm_w):
    b, l, _ = z.shape
    c, hd = GROUP_WIDTH, HEAD_DIM
    lane = np.arange(c)
    ones_bd = jnp.asarray(lane[:, None] // hd == lane[None, :] // hd, BF16)
    expand = np.zeros((2, 128, c), np.float32)
    for h in range(N_HEADS):
        expand[0, h, h * hd:(h + 1) * hd] = 1.0
        expand[1, N_HEADS + h, h * hd:(h + 1) * hd] = 1.0
    ltri = jnp.asarray(np.tril(np.ones((CHUNK, CHUNK), np.float32)), BF16)
    pad = lambda v: jnp.zeros((1, 128), F32).at[0, :N_HEADS].set(v)
    st_spec = pl.BlockSpec((1, N_HEADS, hd, hd), lambda i: (i, 0, 0, 0))
    hist_spec = pl.BlockSpec((1, DELTA_CONV - 1, 3 * c), lambda i: (i, 0, 0))
    return pl.pallas_call(
        _delta_kernel,
        grid=(b,),
        in_specs=[pl.BlockSpec((1, l, 3 * c), lambda i: (i, 0, 0)), _zspec('dz', l),
                  pl.BlockSpec((1, l, 128), lambda i: (i, 0, Z_AB // 128)), hist_spec, st_spec,
                  _full((DELTA_CONV, 3 * c)), _full((1, 128)), _full((1, 128)), _full((1, c)),
                  _full((2, 128, c)), _full((c, c)), _full((CHUNK, CHUNK)), _full((CHUNK, CHUNK))],
        out_specs=[pl.BlockSpec((1, l, c), lambda i: (i, 0, 0)), hist_spec, st_spec],
        out_shape=[jax.ShapeDtypeStruct((b, l, c), F32), jax.ShapeDtypeStruct(hist.shape, F32),
                   jax.ShapeDtypeStruct(s0.shape, F32)],
        scratch_shapes=[pltpu.VMEM((l + 8, 3 * c), F32)] + [pltpu.VMEM((l, c), F32)] * 6
                       + [pltpu.VMEM((N_HEADS, hd, hd), F32)],
        compiler_params=_cparams("arbitrary"),
        name="delta_mixer",
    )(z, z, z, hist, s0, conv_w, pad(a_log), pad(dt_bias), jnp.tile(norm_w, N_HEADS).reshape(1, c),
      jnp.asarray(expand, BF16), ones_bd, ltri, ltri.T)


DEC_TOKENS = 64
STATE_FLAT = HEAD_DIM * HEAD_DIM


def _state_tables():
    hd = HEAD_DIM
    col = np.arange(STATE_FLAT)
    rep = (col[None, :] // hd == np.arange(hd)[:, None]).astype(np.float32)
    tile = (col[None, :] % hd == np.arange(hd)[:, None]).astype(np.float32)
    return jnp.asarray(rep, BF16), jnp.asarray(tile, BF16), jnp.asarray(tile.T, BF16)


def _state_step(s_ref, so_ref, h, w_read, q_read, k_write, decay, u, aqk, rep, tile, red):
    s = s_ref[:, h, :]
    v_new = u if w_read is None else u - _dot3(_dot3(w_read, rep) * s, red)
    o = _dot3(_dot3(q_read, rep) * s, red) + aqk * v_new
    so_ref[:, h, :] = s * _dot3(decay, rep) + _dot3(k_write, rep) * _dot3(v_new, tile)
    return o


def _dec_delta_kernel(z_ref, hist_ref, s_ref, cw_ref, alog_ref, dtb_ref, nw_ref, exp_ref, ones_ref, rep_ref, tile_ref,
                      red_ref, o_ref, nh_ref, so_ref):
    hd, gw = HEAD_DIM, GROUP_WIDTH
    qkv = z_ref[:, 0:3 * gw]
    acc = qkv * cw_ref[DELTA_CONV - 1:DELTA_CONV, :]
    for w in range(DELTA_CONV - 1):
        acc = acc + hist_ref[:, w * 3 * gw:(w + 1) * 3 * gw] * cw_ref[w:w + 1, :]
    nh_ref[:, 0:2 * 3 * gw] = hist_ref[:, 3 * gw:3 * 3 * gw]
    nh_ref[:, 2 * 3 * gw:3 * 3 * gw] = qkv
    act = acc * jax.nn.sigmoid(acc)
    q, k, v = act[:, 0:gw], act[:, gw:2 * gw], act[:, 2 * gw:3 * gw]
    q = q * lax.rsqrt(_dot3(q * q, ones_ref[...]) + EPS) * (hd ** -0.5)
    k = k * lax.rsqrt(_dot3(k * k, ones_ref[...]) + EPS)
    zab = z_ref[:, Z_AB:Z_AB + 128]
    x = zab + dtb_ref[...]
    softplus = jnp.maximum(x, 0.0) + jnp.log(1.0 + jnp.exp(-jnp.abs(x)))
    eg = jnp.exp(_dot3(-jnp.exp(alog_ref[...]) * softplus, exp_ref[0]))
    beta = _dot3(jax.nn.sigmoid(zab), exp_ref[1])
    aqk = _dot3(q * k, ones_ref[...])
    kb = k * beta
    outs = []
    for h in range(N_HEADS):
        ls = slice(h * hd, (h + 1) * hd)
        outs.append(_state_step(s_ref, so_ref, h, (kb * eg)[:, ls], (q * eg)[:, ls], k[:, ls], eg[:, ls],
                                (v * beta)[:, ls], aqk[:, ls], rep_ref[...], tile_ref[...], red_ref[...]))
    o = jnp.concatenate(outs, axis=1)
    ms = _dot3(o * o, ones_ref[...]) * (1.0 / hd)
    dz = z_ref[:, 3 * gw:4 * gw]
    o_ref[...] = o * lax.rsqrt(ms + EPS) * nw_ref[...] * (dz * jax.nn.sigmoid(dz))


def _dec_ret_kernel(z_ref, s_ref, cos_ref, sin_ref, perm_ref, eg_ref, ones_ref, rep_ref, tile_ref, red_ref,
                    o_ref, so_ref):
    hd, gw = HEAD_DIM, GROUP_WIDTH
    q = z_ref[:, 5 * gw:6 * gw]
    k = z_ref[:, 6 * gw:7 * gw]
    v = z_ref[:, 7 * gw:8 * gw]
    g = z_ref[:, 8 * gw:9 * gw]
    q = q * cos_ref[...] + _dot3(q, perm_ref[...]) * sin_ref[...]
    k = (k * cos_ref[...] + _dot3(k, perm_ref[...]) * sin_ref[...]) * (hd ** -0.5)
    aqk = _dot3(q * k, ones_ref[...])
    eg = jnp.broadcast_to(eg_ref[...], q.shape)
    outs = []
    for h in range(N_HEADS):
        ls = slice(h * hd, (h + 1) * hd)
        outs.append(_state_step(s_ref, so_ref, h, None, (q * eg)[:, ls], k[:, ls], eg[:, ls], v[:, ls], aqk[:, ls],
                                rep_ref[...], tile_ref[...], red_ref[...]))
    o = jnp.concatenate(outs, axis=1)
    d = o - _dot3(o, ones_ref[...]) * (1.0 / hd)
    var = _dot3(d * d, ones_ref[...]) * (1.0 / hd)
    o_ref[...] = g * jax.nn.sigmoid(g) * (d * lax.rsqrt(var + EPS))


def _dec_pool_conf_kernel(z_ref, ph_ref, ch_ref, wbd_ref, scale_ref, dw_ref, db_ref, lnw_ref, lnb_ref, pw_ref,
                          po_ref, pnh_ref, co_ref, cnh_ref, *, start_pos):
    gw = GROUP_WIDTH
    u = z_ref[:, 4 * gw:5 * gw]
    lane = lax.broadcasted_iota(jnp.int32, u.shape, 1)
    s = u
    mean = None
    for back in range(1, max(POOL_WINDOWS)):
        s = s + ph_ref[:, (POOL_BUF - back) * gw:(POOL_BUF - back + 1) * gw]
        if back + 1 in POOL_WINDOWS:
            gi = POOL_WINDOWS.index(back + 1)
            m = s / float(min(back + 1, start_pos + 1))
            mean = m if mean is None else jnp.where(lane >= gi * POOL_GROUP, m, mean)
    po_ref[...] = _dot((mean - u).astype(BF16), wbd_ref[...], 1, 0) * scale_ref[...]
    pnh_ref[:, 0:(POOL_BUF - 1) * gw] = ph_ref[:, gw:POOL_BUF * gw]
    pnh_ref[:, (POOL_BUF - 1) * gw:POOL_BUF * gw] = u

    nh = CONF_CONV - 1
    glu = z_ref[:, 9 * gw:10 * gw] * jax.nn.sigmoid(z_ref[:, 10 * gw:11 * gw])
    acc = glu * dw_ref[nh:nh + 1, :] + db_ref[...]
    for w in range(nh):
        acc = acc + ch_ref[:, w * gw:(w + 1) * gw] * dw_ref[w:w + 1, :]
    mu = jnp.mean(acc, axis=-1, keepdims=True)
    d = acc - mu
    var = jnp.mean(d * d, axis=-1, keepdims=True)
    hn = d * lax.rsqrt(var + EPS) * lnw_ref[...] + lnb_ref[...]
    co_ref[...] = _dot((hn * jax.nn.sigmoid(hn)).astype(BF16), pw_ref[...], 1, 0)
    cnh_ref[:, 0:(nh - 1) * gw] = ch_ref[:, gw:nh * gw]
    cnh_ref[:, (nh - 1) * gw:nh * gw] = glu


def _decode_mixers(z, states, p, start_pos):
    n = z.shape[0]
    s_delta, s_dconv, s_pool, s_ret, s_conv = states
    c, hd, bt = GROUP_WIDTH, HEAD_DIM, DEC_TOKENS
    tok = lambda width: pl.BlockSpec((bt, width), lambda i: (i, 0))
    st_spec = pl.BlockSpec((bt, N_HEADS, STATE_FLAT), lambda i: (i, 0, 0))
    st_shape = jax.ShapeDtypeStruct((n, N_HEADS, STATE_FLAT), F32)
    lane = np.arange(c)
    ones_bd = jnp.asarray(lane[:, None] // hd == lane[None, :] // hd, BF16)
    expand = np.zeros((2, 128, c), np.float32)
    for h in range(N_HEADS):
        expand[0, h, h * hd:(h + 1) * hd] = 1.0
        expand[1, N_HEADS + h, h * hd:(h + 1) * hd] = 1.0
    pad = lambda v: jnp.zeros((1, 128), F32).at[0, :N_HEADS].set(v)
    row = lambda v: v.reshape(1, c)
    rep, tile, red = _state_tables()
    tab_specs = [_full(rep.shape), _full(tile.shape), _full(red.shape)]

    o_a, n_dconv, n_delta = pl.pallas_call(
        _dec_delta_kernel,
        grid=(n // bt,),
        in_specs=[tok(Z_WIDTH), tok(3 * 3 * c), st_spec, _full((DELTA_CONV, 3 * c)), _full((1, 128)), _full((1, 128)),
                  _full((1, c)), _full((2, 128, c)), _full((c, c))] + tab_specs,
        out_specs=[tok(c), tok(3 * 3 * c), st_spec],
        out_shape=[jax.ShapeDtypeStruct((n, c), F32), jax.ShapeDtypeStruct((n, 3 * 3 * c), F32), st_shape],
        compiler_params=_cparams("arbitrary"),
        name="decode_delta",
    )(z, s_dconv.reshape(n, -1), s_delta.reshape(n, N_HEADS, STATE_FLAT), p['delta_conv_w'], pad(p['delta_a_log']),
      pad(p['delta_dt_bias']), row(jnp.tile(p['delta_norm_w'], N_HEADS)), jnp.asarray(expand, BF16), ones_bd,
      rep, tile, red)

    cos, sin, perm, _, _, _, _, _ = _ret_tables(1, start_pos)
    log_gamma = jnp.log1p(-jnp.exp2(-5.0 - jnp.arange(N_HEADS, dtype=F32)))
    eg_row = jnp.repeat(jnp.exp(log_gamma), hd).reshape(1, c)
    o_c, n_ret = pl.pallas_call(
        _dec_ret_kernel,
        grid=(n // bt,),
        in_specs=[tok(Z_WIDTH), st_spec, _full((1, c)), _full((1, c)), _full((c, c)), _full((1, c)), _full((c, c))]
                 + tab_specs,
        out_specs=[tok(c), st_spec],
        out_shape=[jax.ShapeDtypeStruct((n, c), F32), st_shape],
        compiler_params=_cparams("arbitrary"),
        name="decode_retention",
    )(z, s_ret.reshape(n, N_HEADS, STATE_FLAT), cos, sin, perm, eg_row, ones_bd, rep, tile, red)

    o_b, n_pool, o_d, n_conv = pl.pallas_call(
        functools.partial(_dec_pool_conf_kernel, start_pos=start_pos),
        grid=(n // bt,),
        in_specs=[tok(Z_WIDTH), tok(POOL_BUF * c), tok((CONF_CONV - 1) * c), _full((c, c)), _full((1, c)),
                  _full((CONF_CONV, c)), _full((1, c)), _full((1, c)), _full((1, c)), _full((c, c))],
        out_specs=[tok(c), tok(POOL_BUF * c), tok(c), tok((CONF_CONV - 1) * c)],
        out_shape=[jax.ShapeDtypeStruct((n, c), F32), jax.ShapeDtypeStruct((n, POOL_BUF * c), F32),
                   jax.ShapeDtypeStruct((n, c), F32), jax.ShapeDtypeStruct((n, (CONF_CONV - 1) * c), F32)],
        compiler_params=_cparams("arbitrary"),
        name="decode_pool_conformer",
    )(z, s_pool.reshape(n, -1), s_conv.reshape(n, -1), p['pool_wbd_bf'], row(p['pool_scale']), p['conv_dw_w'],
      row(p['conv_dw_b']), row(p['conv_ln_w']), row(p['conv_ln_b']), p['conv_pw_bf'])

    mix = jnp.concatenate([o_a, o_b, o_c, o_d], axis=-1)
    new = (n_delta.reshape(s_delta.shape), n_dconv.reshape(s_dconv.shape), n_pool.reshape(s_pool.shape),
           n_ret.reshape(s_ret.shape), n_conv.reshape(s_conv.shape))
    return mix, new


def _pool_blockdiag(pool_w):
    out = jnp.zeros((GROUP_WIDTH, GROUP_WIDTH), pool_w.dtype)
    for gi in range(len(POOL_WINDOWS)):
        sl = slice(gi * POOL_GROUP, (gi + 1) * POOL_GROUP)
        out = out.at[sl, sl].set(pool_w[gi])
    return out


def _permute_w_in(w_in):
    pad = jnp.zeros((w_in.shape[0], Z_WIDTH - Z_AB - 8), w_in.dtype)
    return jnp.concatenate([w_in[:, :OFF_DA], w_in[:, OFF_PU:], w_in[:, OFF_DA:OFF_PU], pad], axis=1)


def _run_trunk(x, states, params, final_norm, start_pos):
    b, l, _ = x.shape
    n = b * l
    xf = x.reshape(n, D_MODEL)
    s_delta, s_dconv, s_pool, s_ret, s_conv = states
    new = ([], [], [], [], [])
    for li in range(DEPTH):
        p = {name: arr[li] for name, arr in params.items()}
        z = _in_proj(xf, p['norm1'].reshape(1, D_MODEL), p['w_in_bf'])
        if l == 1:
            mix, (n_delta, n_dconv, n_pool, n_ret, n_conv) = _decode_mixers(
                z, (s_delta[li], s_dconv[li], s_pool[li], s_ret[li], s_conv[li]), p, start_pos)
        else:
            assert l % ROW_CHUNK == 0, "sequence kernels process whole ROW_CHUNK blocks"
            z = z.reshape(b, l, Z_WIDTH)
            o_a, n_dconv, n_delta = _delta_pallas(z, s_dconv[li], s_delta[li], p['delta_conv_w'], p['delta_a_log'],
                                                  p['delta_dt_bias'], p['delta_norm_w'])
            o_b, n_pool = _pool_pallas(z, s_pool[li], p['pool_wbd_bf'], p['pool_scale'], start_pos)
            o_c, n_ret = _ret_pallas(z, s_ret[li], start_pos)
            o_d, n_conv = _conf_pallas(z, s_conv[li], p['conv_dw_w'], p['conv_dw_b'], p['conv_ln_w'],
                                       p['conv_ln_b'], p['conv_pw_bf'])
            mix = jnp.concatenate([o_a, o_b, o_c, o_d], axis=-1).reshape(n, D_MODEL)
        xf, hn = _out_proj(xf, mix, p['w_out_bf'], p['norm2'].reshape(1, D_MODEL))
        xf = peer_block(hn, xf, p['wqt_bf'], p['keys_bf'], p['u_bf'], p['v_bf'], final_norm.reshape(1, D_MODEL),
                        li == DEPTH - 1)
        for acc, a in zip(new, (n_delta, n_dconv, n_pool, n_ret, n_conv)):
            acc.append(a)
    return xf.reshape(b, l, D_MODEL), tuple(jnp.stack(acc) for acc in new)


def kernel(x_prompt, x_sample, state_delta, state_delta_conv, state_pool, state_ret, state_conv, norm1, w_in, delta_conv_w, delta_a_log, delta_dt_bias, delta_norm_w, pool_w, pool_scale, conv_dw_w, conv_dw_b, conv_ln_w, conv_ln_b, conv_pw_w, w_out, norm2, peer_wq, peer_keys, peer_u, peer_v, final_norm):
    params = {
        'norm1': norm1, 'delta_conv_w': delta_conv_w, 'delta_a_log': delta_a_log, 'delta_dt_bias': delta_dt_bias,
        'delta_norm_w': delta_norm_w, 'pool_w': pool_w, 'pool_scale': pool_scale, 'conv_dw_w': conv_dw_w,
        'conv_dw_b': conv_dw_b, 'conv_ln_w': conv_ln_w, 'conv_ln_b': conv_ln_b, 'conv_pw_w': conv_pw_w,
        'norm2': norm2,
        'w_in_bf': jax.vmap(_permute_w_in)(w_in).astype(BF16),
        'w_out_bf': w_out.astype(BF16),
        'pool_wbd_bf': jax.vmap(_pool_blockdiag)(pool_w).astype(BF16),
        'conv_pw_bf': conv_pw_w.astype(BF16),
        'wqt_bf': jnp.swapaxes(peer_wq, 1, 2).astype(BF16),
        'keys_bf': peer_keys.reshape(DEPTH, 2 * PEER_HEADS * PEER_NKEYS, PEER_DHALF).astype(BF16),
        'u_bf': peer_u.astype(BF16),
        'v_bf': peer_v.astype(BF16),
    }
    bp = x_prompt.shape[0]
    zeros = tuple(jnp.zeros((DEPTH, bp) + s.shape[2:], F32)
                  for s in (state_delta, state_delta_conv, state_pool, state_ret, state_conv))
    y_p, st_p = _run_trunk(x_prompt, zeros, params, final_norm, 0)
    y_s, st_s = _run_trunk(x_sample, (state_delta, state_delta_conv, state_pool, state_ret, state_conv),
                           params, final_norm, PAST_LEN)
    return (y_p, y_s) + st_p + st_s
```

```python
import functools

import numpy as np
import jax
import jax.numpy as jnp
from jax import lax
from jax.experimental import pallas as pl
from jax.experimental.pallas import tpu as pltpu

F32 = jnp.float32
BF16 = jnp.bfloat16

D_MODEL = 1024
DEPTH = 2
HEAD_DIM = 64
GROUP_WIDTH = 256
N_HEADS = 4
POOL_WINDOWS = (2, 4, 8, 16)
POOL_GROUP = 64
POOL_BUF = 15
DELTA_CONV = 4
CONF_CONV = 31
CHUNK = 64
ROPE_BASE = 10000.0
EPS = 1e-6
PAST_LEN = 16384

OFF_DQKV = 0
OFF_DZ = 768
OFF_DA = 1024
OFF_DB = 1028
OFF_PU = 1032
OFF_RQ = 1288
OFF_RK = 1544
OFF_RV = 1800
OFF_RG = 2056
OFF_CG = 2312
IN_WIDTH = 2824
Z_AB = 2816
Z_WIDTH = 2944

PEER_HEADS = 8
PEER_NKEYS = 128
PEER_TOPK = 16
PEER_DHALF = 128
PEER_EXPERTS = PEER_NKEYS * PEER_NKEYS

VMEM_LIMIT = 56 * 1024 * 1024
PEER_TOPK_TOKENS = 256
PEER_EXPERT_TOKENS = 512
PEER_ROWS_PER_CHUNK = 8

NEG_INF = float("-inf")
NOT_RANKED = 99.0

PEER_CANDS = tuple((k, l) for k in range(1, PEER_TOPK + 1) for l in range(1, PEER_TOPK + 1) if k * l <= PEER_TOPK)


def _cparams(*sem):
    return pltpu.CompilerParams(dimension_semantics=sem, vmem_limit_bytes=VMEM_LIMIT)


def _dot(a, b, ca, cb, precision=None):
    return lax.dot_general(a, b, (((ca,), (cb,)), ((), ())), preferred_element_type=F32, precision=precision)


def _in_proj_kernel(x_ref, nw_ref, w_ref, z_ref):
    x = x_ref[...]
    h = x * lax.rsqrt(jnp.mean(x * x, axis=-1, keepdims=True) + EPS) * nw_ref[...]
    z_ref[...] = _dot(h.astype(BF16), w_ref[...], 1, 0)


def _in_proj(x, nw, w_bf):
    n = x.shape[0]
    tm = min(256, n)
    return pl.pallas_call(
        _in_proj_kernel,
        grid=(n // tm,),
        in_specs=[pl.BlockSpec((tm, D_MODEL), lambda i: (i, 0)),
                  pl.BlockSpec((1, D_MODEL), lambda i: (0, 0)),
                  pl.BlockSpec(w_bf.shape, lambda i: (0, 0))],
        out_specs=pl.BlockSpec((tm, w_bf.shape[1]), lambda i: (i, 0)),
        out_shape=jax.ShapeDtypeStruct((n, w_bf.shape[1]), F32),
        compiler_params=_cparams("arbitrary"),
        name="in_proj",
    )(x, nw, w_bf)


def _out_proj_kernel(x_ref, mix_ref, w_ref, nw_ref, xo_ref, hn_ref):
    y = x_ref[...] + _dot(mix_ref[...].astype(BF16), w_ref[...], 1, 0)
    xo_ref[...] = y
    hn = y * lax.rsqrt(jnp.mean(y * y, axis=-1, keepdims=True) + EPS) * nw_ref[...]
    hn_ref[...] = hn.astype(BF16)


def _out_proj(x, mix, w_bf, nw):
    n = x.shape[0]
    tm = min(256, n)
    tok = pl.BlockSpec((tm, D_MODEL), lambda i: (i, 0))
    return pl.pallas_call(
        _out_proj_kernel,
        grid=(n // tm,),
        in_specs=[tok, tok, pl.BlockSpec(w_bf.shape, lambda i: (0, 0)), pl.BlockSpec((1, D_MODEL), lambda i: (0, 0))],
        out_specs=[tok, tok],
        out_shape=[jax.ShapeDtypeStruct((n, D_MODEL), F32), jax.ShapeDtypeStruct((n, D_MODEL), BF16)],
        compiler_params=_cparams("arbitrary"),
        name="out_proj",
    )(x, mix, w_bf, nw)


def _topk_ranks(s):
    nk, t = s.shape
    iota = lax.broadcasted_iota(jnp.int32, (nk, t), 0).astype(F32)
    kiota = lax.broadcasted_iota(jnp.int32, (PEER_TOPK, t), 0)

    def extract(k, carry):
        s, r, tops = carry
        m = jnp.max(s, axis=0, keepdims=True)
        idx = jnp.min(jnp.where(s == m, iota, float(nk)), axis=0, keepdims=True)
        one = iota == idx
        tops = jnp.where(kiota == k, m, tops)
        return jnp.where(one, NEG_INF, s), jnp.where(one, lax.convert_element_type(k + 1, F32), r), tops

    init = (s, jnp.full((nk, t), NOT_RANKED, F32), jnp.zeros((PEER_TOPK, t), F32))
    _, r, tops = lax.fori_loop(0, PEER_TOPK, extract, init)
    return tops, r


def _select_pairs(a, b):
    cand = [a[k - 1] + b[l - 1] for (k, l) in PEER_CANDS]
    sel = [jnp.zeros_like(cand[0]) for _ in PEER_CANDS]
    zsum = jnp.zeros_like(cand[0])
    m0 = functools.reduce(jnp.maximum, cand)
    for _ in range(PEER_TOPK):
        m = functools.reduce(jnp.maximum, cand)
        zsum = zsum + jnp.exp(m - m0)
        found = jnp.zeros_like(m)
        for ci in range(len(PEER_CANDS)):
            hit = jnp.where(cand[ci] == m, 1.0 - found, 0.0)
            found = found + hit
            sel[ci] = sel[ci] + hit
            cand[ci] = jnp.where(hit > 0.0, NEG_INF, cand[ci])
    height = []
    for k in range(1, PEER_TOPK + 1):
        hk = jnp.zeros_like(zsum)
        for ci, (kk, _) in enumerate(PEER_CANDS):
            if kk == k:
                hk = hk + sel[ci]
        height.append(hk)
    return height, 1.0 / zsum


def _peer_a_kernel(hn_ref, wqt_ref, keys_ref, e0_ref, lrow_ref, r1_ref, e1_ref, q_scr, s_scr, top_scr):
    ta = hn_ref.shape[0]
    nk = PEER_NKEYS
    q_scr[...] = _dot(wqt_ref[...], hn_ref[...], 1, 1).astype(BF16)

    for hp in range(2 * PEER_HEADS):
        h, p = hp // 2, hp % 2
        hp_rows = slice(hp * nk, (hp + 1) * nk)
        s_all = _dot(keys_ref[hp_rows, :], q_scr[hp_rows, :], 1, 0)
        s_scr[hp_rows, :] = s_all
        for lt in range(ta // 128):
            lanes = slice(lt * 128, (lt + 1) * 128)
            tops, r = _topk_ranks(s_all[:, lanes])
            for k in range(PEER_TOPK):
                top_scr[p, k, h:h + 1, lanes] = tops[k:k + 1, :]
            (lrow_ref if p == 0 else r1_ref)[h * nk:(h + 1) * nk, lanes] = r

    a = [top_scr[0, k] for k in range(PEER_TOPK)]
    b = [top_scr[1, k] for k in range(PEER_TOPK)]
    height, inv_z = _select_pairs(a, b)
    for h in range(PEER_HEADS):
        rows = slice(h * nk, (h + 1) * nk)
        r0 = lrow_ref[rows, :]
        lrow = jnp.zeros_like(r0)
        for k in range(1, PEER_TOPK + 1):
            lrow = jnp.where(r0 == float(k), height[k - 1][h:h + 1, :], lrow)
        lrow_ref[rows, :] = lrow
        s0 = s_scr[2 * h * nk:(2 * h + 1) * nk, :]
        s1 = s_scr[(2 * h + 1) * nk:(2 * h + 2) * nk, :]
        e0_ref[rows, :] = jnp.exp(s0 - a[0][h:h + 1, :]) * inv_z[h:h + 1, :]
        e1_ref[rows, :] = jnp.exp(s1 - b[0][h:h + 1, :])


def _peer_a(hn, wqt, keys2, ta):
    n = hn.shape[0]
    rows = PEER_HEADS * PEER_NKEYS
    out = jax.ShapeDtypeStruct((rows, n), F32)
    fac_spec = pl.BlockSpec((rows, ta), lambda t: (0, t))
    return pl.pallas_call(
        _peer_a_kernel,
        grid=(n // ta,),
        in_specs=[
            pl.BlockSpec((ta, D_MODEL), lambda t: (t, 0)),
            pl.BlockSpec(wqt.shape, lambda t: (0, 0)),
            pl.BlockSpec(keys2.shape, lambda t: (0, 0)),
        ],
        out_specs=[fac_spec] * 4,
        out_shape=[out] * 4,
        scratch_shapes=[
            pltpu.VMEM((2 * rows, ta), BF16),
            pltpu.VMEM((2 * rows, ta), F32),
            pltpu.VMEM((2, PEER_TOPK, PEER_HEADS, ta), F32),
        ],
        compiler_params=_cparams("arbitrary"),
        name="peer_topk",
    )(hn, wqt, keys2)


def _gelu_tanh(x):
    return 0.5 * x * (1.0 + jnp.tanh(0.7978845608028654 * (x + 0.044715 * (x * x * x))))


def _peer_b_kernel(hn_ref, x_ref, e0_ref, lrow_ref, r1_ref, e1_ref, u_ref, v_ref, fw_ref, o_ref, acc_ref, w_scr,
                   *, rows_per_chunk, final_norm):
    c = pl.program_id(1)
    nk = PEER_NKEYS
    t = hn_ref.shape[0]

    @pl.when(c == 0)
    def _():
        acc_ref[...] = jnp.zeros_like(acc_ref)

    w_scr[...] = _gelu_tanh(_dot(u_ref[...], hn_ref[...], 1, 1))

    def per_row(ii, carry):
        i = c * rows_per_chunk + ii
        lrow = [jnp.broadcast_to(lrow_ref[pl.ds(h * nk + i, 1), :], (8, t)) for h in range(PEER_HEADS)]
        e0 = [jnp.broadcast_to(e0_ref[pl.ds(h * nk + i, 1), :], (8, t)) for h in range(PEER_HEADS)]
        for jt in range(nk // 8):
            g = jnp.zeros((8, t), F32)
            for h in range(PEER_HEADS):
                rows = slice(h * nk + jt * 8, h * nk + jt * 8 + 8)
                g = g + jnp.where(r1_ref[rows, :] <= lrow[h], e1_ref[rows, :], 0.0) * e0[h]
            wrow = pl.ds(pl.multiple_of(ii * nk + jt * 8, 8), 8)
            w_scr[wrow, :] = w_scr[wrow, :] * g
        return carry

    lax.fori_loop(0, rows_per_chunk, per_row, 0)
    acc_ref[...] += _dot(w_scr[...].astype(BF16), v_ref[...], 0, 0)

    @pl.when(c == pl.num_programs(1) - 1)
    def _():
        y = x_ref[...] + acc_ref[...]
        if final_norm:
            y = y * lax.rsqrt(jnp.mean(y * y, axis=-1, keepdims=True) + EPS) * fw_ref[...]
        o_ref[...] = y


def _peer_b(hn, x, facs, u_bf, v_bf, final_w, t, rows_per_chunk, final_norm):
    n = hn.shape[0]
    rows = PEER_HEADS * PEER_NKEYS
    ec = rows_per_chunk * PEER_NKEYS
    fac_spec = pl.BlockSpec((rows, t), lambda ti, c: (0, ti))
    tok_spec = pl.BlockSpec((t, D_MODEL), lambda ti, c: (ti, 0))
    exp_spec = pl.BlockSpec((ec, D_MODEL), lambda ti, c: (c, 0))
    return pl.pallas_call(
        functools.partial(_peer_b_kernel, rows_per_chunk=rows_per_chunk, final_norm=final_norm),
        grid=(n // t, PEER_EXPERTS // ec),
        in_specs=[tok_spec, tok_spec, fac_spec, fac_spec, fac_spec, fac_spec, exp_spec, exp_spec,
                  pl.BlockSpec((1, D_MODEL), lambda ti, c: (0, 0))],
        out_specs=tok_spec,
        out_shape=jax.ShapeDtypeStruct((n, D_MODEL), F32),
        scratch_shapes=[pltpu.VMEM((t, D_MODEL), F32), pltpu.VMEM((ec, t), F32)],
        compiler_params=_cparams("arbitrary", "arbitrary"),
        name="peer_experts",
    )(hn, x, *facs, u_bf, v_bf, final_w)


def peer_block(hn, x, wqt, keys2, u_bf, v_bf, final_w, final_norm):
    n = hn.shape[0]
    facs = _peer_a(hn, wqt, keys2, min(PEER_TOPK_TOKENS, n))
    return _peer_b(hn, x, facs, u_bf, v_bf, final_w, min(PEER_EXPERT_TOKENS, n), PEER_ROWS_PER_CHUNK, final_norm)


ROW_CHUNK = 256
Z_COL = {'dz': 3, 'pu': 4, 'rq': 5, 'rk': 6, 'rv': 7, 'rg': 8, 'ca': 9, 'cg': 10}


def _zspec(name, l):
    return pl.BlockSpec((1, l, GROUP_WIDTH), lambda b, c=Z_COL[name]: (b, 0, c))


def _full(shape):
    return pl.BlockSpec(shape, lambda b: (0,) * len(shape))


def _split3(x):
    hi = x.astype(BF16)
    r = x - hi.astype(F32)
    mid = r.astype(BF16)
    lo = (r - mid.astype(F32)).astype(BF16)
    return hi, mid, lo


def _dot3(x, m_bf):
    hi, mid, lo = _split3(x)
    return _dot(hi, m_bf, 1, 0) + _dot(mid, m_bf, 1, 0) + _dot(lo, m_bf, 1, 0)


def _conf_kernel(a_ref, gt_ref, hist_ref, dw_ref, db_ref, lnw_ref, lnb_ref, pw_ref, o_ref, nh_ref, buf):
    l = a_ref.shape[1]
    off = 32
    buf[0:off - (CONF_CONV - 1), :] = jnp.zeros((off - (CONF_CONV - 1), GROUP_WIDTH), F32)
    buf[off - (CONF_CONV - 1):off, :] = hist_ref[0]
    buf[off:off + l, :] = a_ref[0] * jax.nn.sigmoid(gt_ref[0])
    for r in range(l // ROW_CHUNK):
        base = r * ROW_CHUNK + off - (CONF_CONV - 1)
        acc = jnp.zeros((ROW_CHUNK, GROUP_WIDTH), F32) + db_ref[...]
        for w in range(CONF_CONV):
            acc = acc + buf[base + w:base + w + ROW_CHUNK, :] * dw_ref[w:w + 1, :]
        mu = jnp.mean(acc, axis=-1, keepdims=True)
        d = acc - mu
        var = jnp.mean(d * d, axis=-1, keepdims=True)
        hn = d * lax.rsqrt(var + EPS) * lnw_ref[...] + lnb_ref[...]
        act = hn * jax.nn.sigmoid(hn)
        o_ref[0, r * ROW_CHUNK:(r + 1) * ROW_CHUNK, :] = _dot(act.astype(BF16), pw_ref[...], 1, 0)
    nh_ref[0] = buf[l + off - (CONF_CONV - 1):l + off, :]


def _conf_pallas(z, hist, dw, db, lnw, lnb, pw_bf):
    b, l, _ = z.shape
    c = GROUP_WIDTH
    row = lambda v: v.reshape(1, c)
    return pl.pallas_call(
        _conf_kernel,
        grid=(b,),
        in_specs=[_zspec('ca', l), _zspec('cg', l), pl.BlockSpec((1, CONF_CONV - 1, c), lambda i: (i, 0, 0)),
                  _full((CONF_CONV, c)), _full((1, c)), _full((1, c)), _full((1, c)), _full((c, c))],
        out_specs=[pl.BlockSpec((1, l, c), lambda i: (i, 0, 0)), pl.BlockSpec((1, CONF_CONV - 1, c), lambda i: (i, 0, 0))],
        out_shape=[jax.ShapeDtypeStruct((b, l, c), F32), jax.ShapeDtypeStruct((b, CONF_CONV - 1, c), F32)],
        scratch_shapes=[pltpu.VMEM((l + 32, c), F32)],
        compiler_params=_cparams("arbitrary"),
        name="conformer_mixer",
    )(z, z, hist, dw, row(db), row(lnw), row(lnb), pw_bf)


def _pool_kernel(u_ref, hist_ref, wbd_ref, scale_ref, o_ref, nh_ref, buf, *, start_pos):
    l = u_ref.shape[1]
    off = 16
    buf[0:off - POOL_BUF, :] = jnp.zeros((off - POOL_BUF, GROUP_WIDTH), F32)
    buf[off - POOL_BUF:off, :] = hist_ref[0]
    buf[off:off + l, :] = u_ref[0]
    lane = lax.broadcasted_iota(jnp.int32, (ROW_CHUNK, GROUP_WIDTH), 1)
    rowi = lax.broadcasted_iota(jnp.int32, (ROW_CHUNK, GROUP_WIDTH), 0)
    for r in range(l // ROW_CHUNK):
        base = r * ROW_CHUNK + off
        avail = (rowi + (r * ROW_CHUNK + start_pos + 1)).astype(F32)
        cur = buf[base:base + ROW_CHUNK, :]
        s = cur
        mean = None
        for back in range(1, max(POOL_WINDOWS)):
            s = s + buf[base - back:base - back + ROW_CHUNK, :]
            if back + 1 in POOL_WINDOWS:
                gi = POOL_WINDOWS.index(back + 1)
                m = s / jnp.minimum(float(back + 1), avail)
                mean = m if mean is None else jnp.where(lane >= gi * POOL_GROUP, m, mean)
        p = mean - cur
        y = _dot(p.astype(BF16), wbd_ref[...], 1, 0) * scale_ref[...]
        o_ref[0, r * ROW_CHUNK:(r + 1) * ROW_CHUNK, :] = y
    nh_ref[0] = buf[l + off - POOL_BUF:l + off, :]


def _pool_pallas(z, hist, wbd_bf, scale, start_pos):
    b, l, _ = z.shape
    c = GROUP_WIDTH
    return pl.pallas_call(
        functools.partial(_pool_kernel, start_pos=start_pos),
        grid=(b,),
        in_specs=[_zspec('pu', l), pl.BlockSpec((1, POOL_BUF, c), lambda i: (i, 0, 0)), _full((c, c)), _full((1, c))],
        out_specs=[pl.BlockSpec((1, l, c), lambda i: (i, 0, 0)), pl.BlockSpec((1, POOL_BUF, c), lambda i: (i, 0, 0))],
        out_shape=[jax.ShapeDtypeStruct((b, l, c), F32), jax.ShapeDtypeStruct((b, POOL_BUF, c), F32)],
        scratch_shapes=[pltpu.VMEM((l + 16, c), F32)],
        compiler_params=_cparams("arbitrary"),
        name="pool_mixer",
    )(z, hist, wbd_bf, scale.reshape(1, c))


def _ret_kernel(q_ref, k_ref, v_ref, g_ref, cos_ref, sin_ref, perm_ref, dmask_ref, eg_ref, kd_ref, dl_ref, bd_ref,
                s0_ref, o_ref, sn_ref, o_scr, s_scr):
    l = q_ref.shape[1]
    hd = HEAD_DIM
    s_scr[...] = s0_ref[0]

    def per_chunk(ci, carry):
        rows = pl.ds(pl.multiple_of(ci * CHUNK, CHUNK), CHUNK)
        cos = cos_ref[rows, :]
        sin = sin_ref[rows, :]
        q = q_ref[0, rows, :]
        k = k_ref[0, rows, :]
        q = q * cos + _dot3(q, perm_ref[...]) * sin
        k = (k * cos + _dot3(k, perm_ref[...]) * sin) * (hd ** -0.5)
        v = v_ref[0, rows, :]
        for h in range(N_HEADS):
            ls = slice(h * hd, (h + 1) * hd)
            qh, kh, vh = q[:, ls], k[:, ls], v[:, ls].astype(BF16)
            s_h = s_scr[h]
            aqk = _dot(qh.astype(BF16), kh.astype(BF16), 1, 1) * dmask_ref[h]
            o = _dot((qh * eg_ref[h]).astype(BF16), s_h.astype(BF16), 1, 0) + _dot(aqk.astype(BF16), vh, 1, 0)
            s_scr[h] = s_h * dl_ref[h] + _dot((kh * kd_ref[h]).astype(BF16), vh, 0, 0)
            o_scr[rows, ls] = o
        return carry

    lax.fori_loop(0, l // CHUNK, per_chunk, 0)
    sn_ref[0] = s_scr[...]
    for r in range(l // ROW_CHUNK):
        rows = slice(r * ROW_CHUNK, (r + 1) * ROW_CHUNK)
        o = o_scr[rows, :]
        d = o - _dot3(o, bd_ref[...])
        var = _dot3(d * d, bd_ref[...])
        g = g_ref[0, rows, :]
        o_ref[0, rows, :] = g * jax.nn.sigmoid(g) * (d * lax.rsqrt(var + EPS))


def _ret_tables(l, start_pos):
    hd, half = HEAD_DIM, HEAD_DIM // 2
    pos = jnp.arange(l, dtype=F32) + start_pos
    inv = 1.0 / (ROPE_BASE ** (jnp.arange(half, dtype=F32) / half))
    ang = pos[:, None] * inv[None, :]
    cos = jnp.tile(jnp.cos(ang), (1, 2 * N_HEADS))
    sin = jnp.tile(jnp.sin(ang), (1, 2 * N_HEADS))
    lane = np.arange(GROUP_WIDTH)
    perm = np.zeros((GROUP_WIDTH, GROUP_WIDTH), np.float32)
    first = (lane % hd) < half
    perm[lane[first] + half, lane[first]] = -1.0
    perm[lane[~first] - half, lane[~first]] = 1.0
    bd = (lane[:, None] // hd == lane[None, :] // hd).astype(np.float32) / hd
    c = min(CHUNK, l)
    log_gamma = jnp.log1p(-jnp.exp2(-5.0 - jnp.arange(N_HEADS, dtype=F32)))
    G = jnp.cumsum(jnp.broadcast_to(log_gamma[:, None], (N_HEADS, c)), axis=-1)
    idx = jnp.arange(c)
    incl = idx[:, None] >= idx[None, :]
    dmask = jnp.exp(jnp.where(incl, G[:, :, None] - G[:, None, :], -jnp.inf))
    ones = jnp.ones((N_HEADS, c, hd), F32)
    eg = jnp.exp(G)[:, :, None] * ones
    kd = jnp.exp(G[:, -1:] - G)[:, :, None] * ones
    dl = jnp.exp(G[:, -1])[:, None, None] * jnp.ones((N_HEADS, hd, hd), F32)
    return cos, sin, jnp.asarray(perm, BF16), dmask, eg, kd, dl, jnp.asarray(bd, BF16)


def _ret_pallas(z, s0, start_pos):
    b, l, _ = z.shape
    c = GROUP_WIDTH
    tabs = _ret_tables(l, start_pos)
    st_spec = pl.BlockSpec((1, N_HEADS, HEAD_DIM, HEAD_DIM), lambda i: (i, 0, 0, 0))
    return pl.pallas_call(
        _ret_kernel,
        grid=(b,),
        in_specs=[_zspec('rq', l), _zspec('rk', l), _zspec('rv', l), _zspec('rg', l)]
                 + [_full(t.shape) for t in tabs] + [st_spec],
        out_specs=[pl.BlockSpec((1, l, c), lambda i: (i, 0, 0)), st_spec],
        out_shape=[jax.ShapeDtypeStruct((b, l, c), F32), jax.ShapeDtypeStruct(s0.shape, F32)],
        scratch_shapes=[pltpu.VMEM((l, c), F32), pltpu.VMEM((N_HEADS, HEAD_DIM, HEAD_DIM), F32)],
        compiler_params=_cparams("arbitrary"),
        name="retention_mixer",
    )(z, z, z, z, *tabs, s0)


def _split2(x):
    hi = x.astype(BF16)
    return hi, (x - hi.astype(F32)).astype(BF16)


def _dotx(a, b):
    ah, al = _split2(a)
    bh, bl = _split2(b)
    return _dot(ah, bh, 1, 0) + (_dot(ah, bl, 1, 0) + _dot(al, bh, 1, 0))


def _bdot(a, b, ca=1, cb=0):
    return _dot(a.astype(BF16), b.astype(BF16), ca, cb)


def _delta_kernel(qkv_ref, dz_ref, zab_ref, hist_ref, s0_ref, cw_ref, alog_ref, dtb_ref, nw_ref, exp_ref, ones_ref,
                  ltri_ref, utri_ref, o_ref, nh_ref, sn_ref, buf, q_scr, k_scr, v_scr, g_scr, b_scr, o_scr, s_scr):
    l = qkv_ref.shape[1]
    hd, gw = HEAD_DIM, GROUP_WIDTH
    off = 8
    nhist = DELTA_CONV - 1
    buf[0:off - nhist, :] = jnp.zeros((off - nhist, 3 * gw), F32)
    buf[off - nhist:off, :] = hist_ref[0]
    buf[off:off + l, :] = qkv_ref[0]
    for r in range(l // ROW_CHUNK):
        rows = slice(r * ROW_CHUNK, (r + 1) * ROW_CHUNK)
        base = r * ROW_CHUNK + off - nhist
        acc = jnp.zeros((ROW_CHUNK, 3 * gw), F32)
        for w in range(DELTA_CONV):
            acc = acc + buf[base + w:base + w + ROW_CHUNK, :] * cw_ref[w:w + 1, :]
        act = acc * jax.nn.sigmoid(acc)
        q, k = act[:, 0:gw], act[:, gw:2 * gw]
        q_scr[rows, :] = q * lax.rsqrt(_dot3(q * q, ones_ref[...]) + EPS) * (hd ** -0.5)
        k_scr[rows, :] = k * lax.rsqrt(_dot3(k * k, ones_ref[...]) + EPS)
        v_scr[rows, :] = act[:, 2 * gw:3 * gw]
        zab = zab_ref[0, rows, :]
        x = zab + dtb_ref[...]
        softplus = jnp.maximum(x, 0.0) + jnp.log(1.0 + jnp.exp(-jnp.abs(x)))
        g_scr[rows, :] = _dot3(-jnp.exp(alog_ref[...]) * softplus, exp_ref[0])
        b_scr[rows, :] = _dot3(jax.nn.sigmoid(zab), exp_ref[1])
    nh_ref[0] = buf[l + off - nhist:l + off, :]

    s_scr[...] = s0_ref[0]
    ri = lax.broadcasted_iota(jnp.int32, (CHUNK, CHUNK), 0)
    ci_ = lax.broadcasted_iota(jnp.int32, (CHUNK, CHUNK), 1)
    incl = ri >= ci_
    strict = ri > ci_
    eye = jnp.where(ri == ci_, 1.0, 0.0).astype(F32)
    ltri = ltri_ref[...]
    utri = utri_ref[...]

    def per_chunk(ci, carry):
        rows = pl.ds(pl.multiple_of(ci * CHUNK, CHUNK), CHUNK)
        q, k, v = q_scr[rows, :], k_scr[rows, :], v_scr[rows, :]
        gx, bx = g_scr[rows, :], b_scr[rows, :]
        for h in range(N_HEADS):
            ls = slice(h * hd, (h + 1) * hd)
            qh, kh, vh, gh, bh = q[:, ls], k[:, ls], v[:, ls], gx[:, ls], bx[:, ls]
            g3 = _split3(gh)
            gcol = sum(_dot(ltri, p, 1, 0) for p in g3)
            grow = sum(_dot(p, utri, 0, 0) for p in g3)
            dmask = jnp.where(incl, jnp.exp(gcol - grow), 0.0)
            kb = kh * bh
            a = jnp.where(strict, _bdot(kb, kh, 1, 1) * dmask, 0.0)
            t = eye - a
            pw = a
            for _ in range(5):
                pw = _dotx(pw, pw)
                t = t + _dotx(t, pw)
            eg = jnp.exp(gcol)
            u = _bdot(t, vh * bh)
            wm = _bdot(t, kb * eg)
            aqk = _bdot(qh, kh, 1, 1) * dmask
            glast = gcol[CHUNK - 1:CHUNK, :]
            kd = kh * jnp.exp(glast - gcol)
            s_h = s_scr[h]
            v_new = u - _bdot(wm, s_h)
            o_scr[rows, ls] = _bdot(qh * eg, s_h) + _bdot(aqk, v_new)
            s_scr[h] = s_h * jnp.exp(glast) + _bdot(kd, v_new, 0, 0)
        return carry

    lax.fori_loop(0, l // CHUNK, per_chunk, 0)
    sn_ref[0] = s_scr[...]
    for r in range(l // ROW_CHUNK):
        rows = slice(r * ROW_CHUNK, (r + 1) * ROW_CHUNK)
        o = o_scr[rows, :]
        ms = _dot3(o * o, ones_ref[...]) * (1.0 / hd)
        dz = dz_ref[0, rows, :]
        o_ref[0, rows, :] = o * lax.rsqrt(ms + EPS) * nw_ref[...] * (dz * jax.nn.sigmoid(dz))


def _delta_pallas(z, hist, s0, conv_w, a_log, dt_bias, norm_w):
    b, l, _ = z.shape
    c, hd = GROUP_WIDTH, HEAD_DIM
    lane = np.arange(c)
    ones_bd = jnp.asarray(lane[:, None] // hd == lane[None, :] // hd, BF16)
    expand = np.zeros((2, 128, c), np.float32)
    for h in range(N_HEADS):
        expand[0, h, h * hd:(h + 1) * hd] = 1.0
        expand[1, N_HEADS + h, h * hd:(h + 1) * hd] = 1.0
    ltri = jnp.asarray(np.tril(np.ones((CHUNK, CHUNK), np.float32)), BF16)
    pad = lambda v: jnp.zeros((1, 128), F32).at[0, :N_HEADS].set(v)
    st_spec = pl.BlockSpec((1, N_HEADS, hd, hd), lambda i: (i, 0, 0, 0))
    hist_spec = pl.BlockSpec((1, DELTA_CONV - 1, 3 * c), lambda i: (i, 0, 0))
    return pl.pallas_call(
        _delta_kernel,
        grid=(b,),
        in_specs=[pl.BlockSpec((1, l, 3 * c), lambda i: (i, 0, 0)), _zspec('dz', l),
                  pl.BlockSpec((1, l, 128), lambda i: (i, 0, Z_AB // 128)), hist_spec, st_spec,
                  _full((DELTA_CONV, 3 * c)), _full((1, 128)), _full((1, 128)), _full((1, c)),
                  _full((2, 128, c)), _full((c, c)), _full((CHUNK, CHUNK)), _full((CHUNK, CHUNK))],
        out_specs=[pl.BlockSpec((1, l, c), lambda i: (i, 0, 0)), hist_spec, st_spec],
        out_shape=[jax.ShapeDtypeStruct((b, l, c), F32), jax.ShapeDtypeStruct(hist.shape, F32),
                   jax.ShapeDtypeStruct(s0.shape, F32)],
        scratch_shapes=[pltpu.VMEM((l + 8, 3 * c), F32)] + [pltpu.VMEM((l, c), F32)] * 6
                       + [pltpu.VMEM((N_HEADS, hd, hd), F32)],
        compiler_params=_cparams("arbitrary"),
        name="delta_mixer",
    )(z, z, z, hist, s0, conv_w, pad(a_log), pad(dt_bias), jnp.tile(norm_w, N_HEADS).reshape(1, c),
      jnp.asarray(expand, BF16), ones_bd, ltri, ltri.T)


DEC_TOKENS = 64
STATE_FLAT = HEAD_DIM * HEAD_DIM


def _state_tables():
    hd = HEAD_DIM
    col = np.arange(STATE_FLAT)
    rep = (col[None, :] // hd == np.arange(hd)[:, None]).astype(np.float32)
    tile = (col[None, :] % hd == np.arange(hd)[:, None]).astype(np.float32)
    return jnp.asarray(rep, BF16), jnp.asarray(tile, BF16), jnp.asarray(tile.T, BF16)


def _state_step(s_ref, so_ref, h, w_read, q_read, k_write, decay, u, aqk, rep, tile, red):
    s = s_ref[:, h, :]
    v_new = u if w_read is None else u - _dot3(_dot3(w_read, rep) * s, red)
    o = _dot3(_dot3(q_read, rep) * s, red) + aqk * v_new
    so_ref[:, h, :] = s * _dot3(decay, rep) + _dot3(k_write, rep) * _dot3(v_new, tile)
    return o


def _dec_delta_kernel(z_ref, hist_ref, s_ref, cw_ref, alog_ref, dtb_ref, nw_ref, exp_ref, ones_ref, rep_ref, tile_ref,
                      red_ref, o_ref, nh_ref, so_ref):
    hd, gw = HEAD_DIM, GROUP_WIDTH
    qkv = z_ref[:, 0:3 * gw]
    acc = qkv * cw_ref[DELTA_CONV - 1:DELTA_CONV, :]
    for w in range(DELTA_CONV - 1):
        acc = acc + hist_ref[:, w * 3 * gw:(w + 1) * 3 * gw] * cw_ref[w:w + 1, :]
    nh_ref[:, 0:2 * 3 * gw] = hist_ref[:, 3 * gw:3 * 3 * gw]
    nh_ref[:, 2 * 3 * gw:3 * 3 * gw] = qkv
    act = acc * jax.nn.sigmoid(acc)
    q, k, v = act[:, 0:gw], act[:, gw:2 * gw], act[:, 2 * gw:3 * gw]
    q = q * lax.rsqrt(_dot3(q * q, ones_ref[...]) + EPS) * (hd ** -0.5)
    k = k * lax.rsqrt(_dot3(k * k, ones_ref[...]) + EPS)
    zab = z_ref[:, Z_AB:Z_AB + 128]
    x = zab + dtb_ref[...]
    softplus = jnp.maximum(x, 0.0) + jnp.log(1.0 + jnp.exp(-jnp.abs(x)))
    eg = jnp.exp(_dot3(-jnp.exp(alog_ref[...]) * softplus, exp_ref[0]))
    beta = _dot3(jax.nn.sigmoid(zab), exp_ref[1])
    aqk = _dot3(q * k, ones_ref[...])
    kb = k * beta
    outs = []
    for h in range(N_HEADS):
        ls = slice(h * hd, (h + 1) * hd)
        outs.append(_state_step(s_ref, so_ref, h, (kb * eg)[:, ls], (q * eg)[:, ls], k[:, ls], eg[:, ls],
                                (v * beta)[:, ls], aqk[:, ls], rep_ref[...], tile_ref[...], red_ref[...]))
    o = jnp.concatenate(outs, axis=1)
    ms = _dot3(o * o, ones_ref[...]) * (1.0 / hd)
    dz = z_ref[:, 3 * gw:4 * gw]
    o_ref[...] = o * lax.rsqrt(ms + EPS) * nw_ref[...] * (dz * jax.nn.sigmoid(dz))


def _dec_ret_kernel(z_ref, s_ref, cos_ref, sin_ref, perm_ref, eg_ref, ones_ref, rep_ref, tile_ref, red_ref,
                    o_ref, so_ref):
    hd, gw = HEAD_DIM, GROUP_WIDTH
    q = z_ref[:, 5 * gw:6 * gw]
    k = z_ref[:, 6 * gw:7 * gw]
    v = z_ref[:, 7 * gw:8 * gw]
    g = z_ref[:, 8 * gw:9 * gw]
    q = q * cos_ref[...] + _dot3(q, perm_ref[...]) * sin_ref[...]
    k = (k * cos_ref[...] + _dot3(k, perm_ref[...]) * sin_ref[...]) * (hd ** -0.5)
    aqk = _dot3(q * k, ones_ref[...])
    eg = jnp.broadcast_to(eg_ref[...], q.shape)
    outs = []
    for h in range(N_HEADS):
        ls = slice(h * hd, (h + 1) * hd)
        outs.append(_state_step(s_ref, so_ref, h, None, (q * eg)[:, ls], k[:, ls], eg[:, ls], v[:, ls], aqk[:, ls],
                                rep_ref[...], tile_ref[...], red_ref[...]))
    o = jnp.concatenate(outs, axis=1)
    d = o - _dot3(o, ones_ref[...]) * (1.0 / hd)
    var = _dot3(d * d, ones_ref[...]) * (1.0 / hd)
    o_ref[...] = g * jax.nn.sigmoid(g) * (d * lax.rsqrt(var + EPS))


def _dec_pool_conf_kernel(z_ref, ph_ref, ch_ref, wbd_ref, scale_ref, dw_ref, db_ref, lnw_ref, lnb_ref, pw_ref,
                          po_ref, pnh_ref, co_ref, cnh_ref, *, start_pos):
    gw = GROUP_WIDTH
    u = z_ref[:, 4 * gw:5 * gw]
    lane = lax.broadcasted_iota(jnp.int32, u.shape, 1)
    s = u
    mean = None
    for back in range(1, max(POOL_WINDOWS)):
        s = s + ph_ref[:, (POOL_BUF - back) * gw:(POOL_BUF - back + 1) * gw]
        if back + 1 in POOL_WINDOWS:
            gi = POOL_WINDOWS.index(back + 1)
            m = s / float(min(back + 1, start_pos + 1))
            mean = m if mean is None else jnp.where(lane >= gi * POOL_GROUP, m, mean)
    po_ref[...] = _dot((mean - u).astype(BF16), wbd_ref[...], 1, 0) * scale_ref[...]
    pnh_ref[:, 0:(POOL_BUF - 1) * gw] = ph_ref[:, gw:POOL_BUF * gw]
    pnh_ref[:, (POOL_BUF - 1) * gw:POOL_BUF * gw] = u

    nh = CONF_CONV - 1
    glu = z_ref[:, 9 * gw:10 * gw] * jax.nn.sigmoid(z_ref[:, 10 * gw:11 * gw])
    acc = glu * dw_ref[nh:nh + 1, :] + db_ref[...]
    for w in range(nh):
        acc = acc + ch_ref[:, w * gw:(w + 1) * gw] * dw_ref[w:w + 1, :]
    mu = jnp.mean(acc, axis=-1, keepdims=True)
    d = acc - mu
    var = jnp.mean(d * d, axis=-1, keepdims=True)
    hn = d * lax.rsqrt(var + EPS) * lnw_ref[...] + lnb_ref[...]
    co_ref[...] = _dot((hn * jax.nn.sigmoid(hn)).astype(BF16), pw_ref[...], 1, 0)
    cnh_ref[:, 0:(nh - 1) * gw] = ch_ref[:, gw:nh * gw]
    cnh_ref[:, (nh - 1) * gw:nh * gw] = glu


def _decode_mixers(z, states, p, start_pos):
    n = z.shape[0]
    s_delta, s_dconv, s_pool, s_ret, s_conv = states
    c, hd, bt = GROUP_WIDTH, HEAD_DIM, DEC_TOKENS
    tok = lambda width: pl.BlockSpec((bt, width), lambda i: (i, 0))
    st_spec = pl.BlockSpec((bt, N_HEADS, STATE_FLAT), lambda i: (i, 0, 0))
    st_shape = jax.ShapeDtypeStruct((n, N_HEADS, STATE_FLAT), F32)
    lane = np.arange(c)
    ones_bd = jnp.asarray(lane[:, None] // hd == lane[None, :] // hd, BF16)
    expand = np.zeros((2, 128, c), np.float32)
    for h in range(N_HEADS):
        expand[0, h, h * hd:(h + 1) * hd] = 1.0
        expand[1, N_HEADS + h, h * hd:(h + 1) * hd] = 1.0
    pad = lambda v: jnp.zeros((1, 128), F32).at[0, :N_HEADS].set(v)
    row = lambda v: v.reshape(1, c)
    rep, tile, red = _state_tables()
    tab_specs = [_full(rep.shape), _full(tile.shape), _full(red.shape)]

    o_a, n_dconv, n_delta = pl.pallas_call(
        _dec_delta_kernel,
        grid=(n // bt,),
        in_specs=[tok(Z_WIDTH), tok(3 * 3 * c), st_spec, _full((DELTA_CONV, 3 * c)), _full((1, 128)), _full((1, 128)),
                  _full((1, c)), _full((2, 128, c)), _full((c, c))] + tab_specs,
        out_specs=[tok(c), tok(3 * 3 * c), st_spec],
        out_shape=[jax.ShapeDtypeStruct((n, c), F32), jax.ShapeDtypeStruct((n, 3 * 3 * c), F32), st_shape],
        compiler_params=_cparams("arbitrary"),
        name="decode_delta",
    )(z, s_dconv.reshape(n, -1), s_delta.reshape(n, N_HEADS, STATE_FLAT), p['delta_conv_w'], pad(p['delta_a_log']),
      pad(p['delta_dt_bias']), row(jnp.tile(p['delta_norm_w'], N_HEADS)), jnp.asarray(expand, BF16), ones_bd,
      rep, tile, red)

    cos, sin, perm, _, _, _, _, _ = _ret_tables(1, start_pos)
    log_gamma = jnp.log1p(-jnp.exp2(-5.0 - jnp.arange(N_HEADS, dtype=F32)))
    eg_row = jnp.repeat(jnp.exp(log_gamma), hd).reshape(1, c)
    o_c, n_ret = pl.pallas_call(
        _dec_ret_kernel,
        grid=(n // bt,),
        in_specs=[tok(Z_WIDTH), st_spec, _full((1, c)), _full((1, c)), _full((c, c)), _full((1, c)), _full((c, c))]
                 + tab_specs,
        out_specs=[tok(c), st_spec],
        out_shape=[jax.ShapeDtypeStruct((n, c), F32), st_shape],
        compiler_params=_cparams("arbitrary"),
        name="decode_retention",
    )(z, s_ret.reshape(n, N_HEADS, STATE_FLAT), cos, sin, perm, eg_row, ones_bd, rep, tile, red)

    o_b, n_pool, o_d, n_conv = pl.pallas_call(
        functools.partial(_dec_pool_conf_kernel, start_pos=start_pos),
        grid=(n // bt,),
        in_specs=[tok(Z_WIDTH), tok(POOL_BUF * c), tok((CONF_CONV - 1) * c), _full((c, c)), _full((1, c)),
                  _full((CONF_CONV, c)), _full((1, c)), _full((1, c)), _full((1, c)), _full((c, c))],
        out_specs=[tok(c), tok(POOL_BUF * c), tok(c), tok((CONF_CONV - 1) * c)],
        out_shape=[jax.ShapeDtypeStruct((n, c), F32), jax.ShapeDtypeStruct((n, POOL_BUF * c), F32),
                   jax.ShapeDtypeStruct((n, c), F32), jax.ShapeDtypeStruct((n, (CONF_CONV - 1) * c), F32)],
        compiler_params=_cparams("arbitrary"),
        name="decode_pool_conformer",
    )(z, s_pool.reshape(n, -1), s_conv.reshape(n, -1), p['pool_wbd_bf'], row(p['pool_scale']), p['conv_dw_w'],
      row(p['conv_dw_b']), row(p['conv_ln_w']), row(p['conv_ln_b']), p['conv_pw_bf'])

    mix = jnp.concatenate([o_a, o_b, o_c, o_d], axis=-1)
    new = (n_delta.reshape(s_delta.shape), n_dconv.reshape(s_dconv.shape), n_pool.reshape(s_pool.shape),
           n_ret.reshape(s_ret.shape), n_conv.reshape(s_conv.shape))
    return mix, new


def _pool_blockdiag(pool_w):
    out = jnp.zeros((GROUP_WIDTH, GROUP_WIDTH), pool_w.dtype)
    for gi in range(len(POOL_WINDOWS)):
        sl = slice(gi * POOL_GROUP, (gi + 1) * POOL_GROUP)
        out = out.at[sl, sl].set(pool_w[gi])
    return out


def _permute_w_in(w_in):
    pad = jnp.zeros((w_in.shape[0], Z_WIDTH - Z_AB - 8), w_in.dtype)
    return jnp.concatenate([w_in[:, :OFF_DA], w_in[:, OFF_PU:], w_in[:, OFF_DA:OFF_PU], pad], axis=1)


def _run_trunk(x, states, params, final_norm, start_pos):
    b, l, _ = x.shape
    n = b * l
    xf = x.reshape(n, D_MODEL)
    s_delta, s_dconv, s_pool, s_ret, s_conv = states
    new = ([], [], [], [], [])
    for li in range(DEPTH):
        p = {name: arr[li] for name, arr in params.items()}
        z = _in_proj(xf, p['norm1'].reshape(1, D_MODEL), p['w_in_bf'])
        if l == 1:
            mix, (n_delta, n_dconv, n_pool, n_ret, n_conv) = _decode_mixers(
                z, (s_delta[li], s_dconv[li], s_pool[li], s_ret[li], s_conv[li]), p, start_pos)
        else:
            assert l % ROW_CHUNK == 0, "sequence kernels process whole ROW_CHUNK blocks"
            z = z.reshape(b, l, Z_WIDTH)
            o_a, n_dconv, n_delta = _delta_pallas(z, s_dconv[li], s_delta[li], p['delta_conv_w'], p['delta_a_log'],
                                                  p['delta_dt_bias'], p['delta_norm_w'])
            o_b, n_pool = _pool_pallas(z, s_pool[li], p['pool_wbd_bf'], p['pool_scale'], start_pos)
            o_c, n_ret = _ret_pallas(z, s_ret[li], start_pos)
            o_d, n_conv = _conf_pallas(z, s_conv[li], p['conv_dw_w'], p['conv_dw_b'], p['conv_ln_w'],
                                       p['conv_ln_b'], p['conv_pw_bf'])
            mix = jnp.concatenate([o_a, o_b, o_c, o_d], axis=-1).reshape(n, D_MODEL)
        xf, hn = _out_proj(xf, mix, p['w_out_bf'], p['norm2'].reshape(1, D_MODEL))
        xf = peer_block(hn, xf, p['wqt_bf'], p['keys_bf'], p['u_bf'], p['v_bf'], final_norm.reshape(1, D_MODEL),
                        li == DEPTH - 1)
        for acc, a in zip(new, (n_delta, n_dconv, n_pool, n_ret, n_conv)):
            acc.append(a)
    return xf.reshape(b, l, D_MODEL), tuple(jnp.stack(acc) for acc in new)


def kernel(x_prompt, x_sample, state_delta, state_delta_conv, state_pool, state_ret, state_conv, norm1, w_in, delta_conv_w, delta_a_log, delta_dt_bias, delta_norm_w, pool_w, pool_scale, conv_dw_w, conv_dw_b, conv_ln_w, conv_ln_b, conv_pw_w, w_out, norm2, peer_wq, peer_keys, peer_u, peer_v, final_norm):
    params = {
        'norm1': norm1, 'delta_conv_w': delta_conv_w, 'delta_a_log': delta_a_log, 'delta_dt_bias': delta_dt_bias,
        'delta_norm_w': delta_norm_w, 'pool_scale': pool_scale, 'conv_dw_w': conv_dw_w,
        'conv_dw_b': conv_dw_b, 'conv_ln_w': conv_ln_w, 'conv_ln_b': conv_ln_b,
        'norm2': norm2,
        'w_in_bf': jax.vmap(_permute_w_in)(w_in).astype(BF16),
        'w_out_bf': w_out.astype(BF16),
        'pool_wbd_bf': jax.vmap(_pool_blockdiag)(pool_w).astype(BF16),
        'conv_pw_bf': conv_pw_w.astype(BF16),
        'wqt_bf': jnp.swapaxes(peer_wq, 1, 2).astype(BF16),
        'keys_bf': peer_keys.reshape(DEPTH, 2 * PEER_HEADS * PEER_NKEYS, PEER_DHALF).astype(BF16),
        'u_bf': peer_u.astype(BF16),
        'v_bf': peer_v.astype(BF16),
    }
    bp = x_prompt.shape[0]
    zeros = tuple(jnp.zeros((DEPTH, bp) + s.shape[2:], F32)
                  for s in (state_delta, state_delta_conv, state_pool, state_ret, state_conv))
    y_p, st_p = _run_trunk(x_prompt, zeros, params, final_norm, 0)
    y_s, st_s = _run_trunk(x_sample, (state_delta, state_delta_conv, state_pool, state_ret, state_conv),
                           params, final_norm, PAST_LEN)
    return (y_p, y_s) + st_p + st_s
```

```python
import functools

import numpy as np
import jax
import jax.numpy as jnp
from jax import lax
from jax.experimental import pallas as pl
from jax.experimental.pallas import tpu as pltpu

F32 = jnp.float32
BF16 = jnp.bfloat16

D_MODEL = 1024
DEPTH = 2
HEAD_DIM = 64
GROUP_WIDTH = 256
N_HEADS = 4
POOL_WINDOWS = (2, 4, 8, 16)
POOL_GROUP = 64
POOL_BUF = 15
DELTA_CONV = 4
CONF_CONV = 31
CHUNK = 64
ROPE_BASE = 10000.0
EPS = 1e-6
PAST_LEN = 16384

OFF_DQKV = 0
OFF_DZ = 768
OFF_DA = 1024
OFF_DB = 1028
OFF_PU = 1032
OFF_RQ = 1288
OFF_RK = 1544
OFF_RV = 1800
OFF_RG = 2056
OFF_CG = 2312
IN_WIDTH = 2824
Z_AB = 2816
Z_WIDTH = 2944

PEER_HEADS = 8
PEER_NKEYS = 128
PEER_TOPK = 16
PEER_DHALF = 128
PEER_EXPERTS = PEER_NKEYS * PEER_NKEYS

VMEM_LIMIT = 56 * 1024 * 1024
PEER_TOPK_TOKENS = 256
PEER_EXPERT_TOKENS = 512
PEER_ROWS_PER_CHUNK = 8

NEG_INF = float("-inf")
NOT_RANKED = 99.0

PEER_CANDS = tuple((k, l) for k in range(1, PEER_TOPK + 1) for l in range(1, PEER_TOPK + 1) if k * l <= PEER_TOPK)


def _cparams(*sem):
    return pltpu.CompilerParams(dimension_semantics=sem, vmem_limit_bytes=VMEM_LIMIT)


def _dot(a, b, ca, cb, precision=None):
    return lax.dot_general(a, b, (((ca,), (cb,)), ((), ())), preferred_element_type=F32, precision=precision)


def _in_proj_kernel(x_ref, nw_ref, w_ref, z_ref):
    x = x_ref[...]
    h = x * lax.rsqrt(jnp.mean(x * x, axis=-1, keepdims=True) + EPS) * nw_ref[...]
    z_ref[...] = _dot(h.astype(BF16), w_ref[...], 1, 0)


def _in_proj(x, nw, w_bf):
    n = x.shape[0]
    tm = min(256, n)
    return pl.pallas_call(
        _in_proj_kernel,
        grid=(n // tm,),
        in_specs=[pl.BlockSpec((tm, D_MODEL), lambda i: (i, 0)),
                  pl.BlockSpec((1, D_MODEL), lambda i: (0, 0)),
                  pl.BlockSpec(w_bf.shape, lambda i: (0, 0))],
        out_specs=pl.BlockSpec((tm, w_bf.shape[1]), lambda i: (i, 0)),
        out_shape=jax.ShapeDtypeStruct((n, w_bf.shape[1]), F32),
        compiler_params=_cparams("arbitrary"),
        name="in_proj",
    )(x, nw, w_bf)


def _out_proj_kernel(x_ref, mix_ref, w_ref, nw_ref, xo_ref, hn_ref):
    y = x_ref[...] + _dot(mix_ref[...].astype(BF16), w_ref[...], 1, 0)
    xo_ref[...] = y
    hn = y * lax.rsqrt(jnp.mean(y * y, axis=-1, keepdims=True) + EPS) * nw_ref[...]
    hn_ref[...] = hn.astype(BF16)


def _out_proj(x, mix, w_bf, nw):
    n = x.shape[0]
    tm = min(256, n)
    tok = pl.BlockSpec((tm, D_MODEL), lambda i: (i, 0))
    return pl.pallas_call(
        _out_proj_kernel,
        grid=(n // tm,),
        in_specs=[tok, tok, pl.BlockSpec(w_bf.shape, lambda i: (0, 0)), pl.BlockSpec((1, D_MODEL), lambda i: (0, 0))],
        out_specs=[tok, tok],
        out_shape=[jax.ShapeDtypeStruct((n, D_MODEL), F32), jax.ShapeDtypeStruct((n, D_MODEL), BF16)],
        compiler_params=_cparams("arbitrary"),
        name="out_proj",
    )(x, mix, w_bf, nw)


def _topk_ranks(s):
    nk, t = s.shape
    lw = min(128, t)
    nt = t // lw
    iota = lax.broadcasted_iota(jnp.int32, (nk, lw), 0).astype(F32)
    kiota = lax.broadcasted_iota(jnp.int32, (PEER_TOPK, lw), 0)

    def extract(k, carry):
        out = []
        for s, tops, idxs in carry:
            m = jnp.max(s, axis=0, keepdims=True)
            idx = jnp.min(jnp.where(s == m, iota, float(nk)), axis=0, keepdims=True)
            out.append((jnp.where(iota == idx, NEG_INF, s), jnp.where(kiota == k, m, tops),
                        jnp.where(kiota == k, idx, idxs)))
        return tuple(out)

    init = tuple((s[:, j * lw:(j + 1) * lw], jnp.zeros((PEER_TOPK, lw), F32), jnp.full((PEER_TOPK, lw), -1.0, F32))
                 for j in range(nt))
    res = lax.fori_loop(0, PEER_TOPK, extract, init)
    ranks = []
    for _, _, idxs in res:
        r = jnp.full((nk, lw), NOT_RANKED, F32)
        for k in range(PEER_TOPK):
            r = jnp.where(iota == idxs[k:k + 1, :], float(k + 1), r)
        ranks.append(r)
    tops = [tp for _, tp, _ in res]
    return (tops[0], ranks[0]) if nt == 1 else (jnp.concatenate(tops, axis=1), jnp.concatenate(ranks, axis=1))


def _select_pairs(a, b):
    cand = [a[k - 1] + b[l - 1] for (k, l) in PEER_CANDS]
    sel = [jnp.zeros_like(cand[0]) for _ in PEER_CANDS]
    zsum = jnp.zeros_like(cand[0])
    m0 = functools.reduce(jnp.maximum, cand)
    for _ in range(PEER_TOPK):
        m = functools.reduce(jnp.maximum, cand)
        zsum = zsum + jnp.exp(m - m0)
        found = jnp.zeros_like(m)
        for ci in range(len(PEER_CANDS)):
            hit = jnp.where(cand[ci] == m, 1.0 - found, 0.0)
            found = found + hit
            sel[ci] = sel[ci] + hit
            cand[ci] = jnp.where(hit > 0.0, NEG_INF, cand[ci])
    height = []
    for k in range(1, PEER_TOPK + 1):
        hk = jnp.zeros_like(zsum)
        for ci, (kk, _) in enumerate(PEER_CANDS):
            if kk == k:
                hk = hk + sel[ci]
        height.append(hk)
    return height, 1.0 / zsum


def _peer_a_kernel(hn_ref, wqt_ref, keys_ref, e0_ref, lrow_ref, r1_ref, e1_ref, q_scr, s_scr, top_scr):
    ta = hn_ref.shape[0]
    nk = PEER_NKEYS
    q_scr[...] = _dot(wqt_ref[...], hn_ref[...], 1, 1).astype(BF16)

    for hp in range(2 * PEER_HEADS):
        h, p = hp // 2, hp % 2
        hp_rows = slice(hp * nk, (hp + 1) * nk)
        s_all = _dot(keys_ref[hp_rows, :], q_scr[hp_rows, :], 1, 0)
        s_scr[hp_rows, :] = s_all
        tops, r = _topk_ranks(s_all)
        for k in range(PEER_TOPK):
            top_scr[p, k, h:h + 1, :] = tops[k:k + 1, :]
        (lrow_ref if p == 0 else r1_ref)[h * nk:(h + 1) * nk, :] = r

    a = [top_scr[0, k] for k in range(PEER_TOPK)]
    b = [top_scr[1, k] for k in range(PEER_TOPK)]
    height, inv_z = _select_pairs(a, b)
    for h in range(PEER_HEADS):
        rows = slice(h * nk, (h + 1) * nk)
        r0 = lrow_ref[rows, :]
        lrow = jnp.zeros_like(r0)
        for k in range(1, PEER_TOPK + 1):
            lrow = jnp.where(r0 == float(k), height[k - 1][h:h + 1, :], lrow)
        lrow_ref[rows, :] = lrow
        s0 = s_scr[2 * h * nk:(2 * h + 1) * nk, :]
        s1 = s_scr[(2 * h + 1) * nk:(2 * h + 2) * nk, :]
        e0_ref[rows, :] = jnp.exp(s0 - a[0][h:h + 1, :]) * inv_z[h:h + 1, :]
        e1_ref[rows, :] = jnp.exp(s1 - b[0][h:h + 1, :])


def _peer_a(hn, wqt, keys2, ta):
    n = hn.shape[0]
    rows = PEER_HEADS * PEER_NKEYS
    out = jax.ShapeDtypeStruct((rows, n), F32)
    fac_spec = pl.BlockSpec((rows, ta), lambda t: (0, t))
    return pl.pallas_call(
        _peer_a_kernel,
        grid=(n // ta,),
        in_specs=[
            pl.BlockSpec((ta, D_MODEL), lambda t: (t, 0)),
            pl.BlockSpec(wqt.shape, lambda t: (0, 0)),
            pl.BlockSpec(keys2.shape, lambda t: (0, 0)),
        ],
        out_specs=[fac_spec] * 4,
        out_shape=[out] * 4,
        scratch_shapes=[
            pltpu.VMEM((2 * rows, ta), BF16),
            pltpu.VMEM((2 * rows, ta), F32),
            pltpu.VMEM((2, PEER_TOPK, PEER_HEADS, ta), F32),
        ],
        compiler_params=_cparams("arbitrary"),
        name="peer_topk",
    )(hn, wqt, keys2)


def _gelu_tanh(x):
    return 0.5 * x * (1.0 + jnp.tanh(0.7978845608028654 * (x + 0.044715 * (x * x * x))))


def _peer_b_kernel(hn_ref, x_ref, e0_ref, lrow_ref, r1_ref, e1_ref, u_ref, v_ref, fw_ref, o_ref, acc_ref, w_scr,
                   *, rows_per_chunk, final_norm):
    c = pl.program_id(1)
    nk = PEER_NKEYS
    t = hn_ref.shape[0]

    @pl.when(c == 0)
    def _():
        acc_ref[...] = jnp.zeros_like(acc_ref)

    w_scr[...] = _gelu_tanh(_dot(u_ref[...], hn_ref[...], 1, 1))

    def per_row(ii, carry):
        i = c * rows_per_chunk + ii
        lrow = [jnp.broadcast_to(lrow_ref[pl.ds(h * nk + i, 1), :], (8, t)) for h in range(PEER_HEADS)]
        e0 = [jnp.broadcast_to(e0_ref[pl.ds(h * nk + i, 1), :], (8, t)) for h in range(PEER_HEADS)]
        for jt in range(nk // 8):
            g = jnp.zeros((8, t), F32)
            for h in range(PEER_HEADS):
                rows = slice(h * nk + jt * 8, h * nk + jt * 8 + 8)
                g = g + jnp.where(r1_ref[rows, :] <= lrow[h], e1_ref[rows, :], 0.0) * e0[h]
            wrow = pl.ds(pl.multiple_of(ii * nk + jt * 8, 8), 8)
            w_scr[wrow, :] = w_scr[wrow, :] * g
        return carry

    lax.fori_loop(0, rows_per_chunk, per_row, 0)
    acc_ref[...] += _dot(w_scr[...].astype(BF16), v_ref[...], 0, 0)

    @pl.when(c == pl.num_programs(1) - 1)
    def _():
        y = x_ref[...] + acc_ref[...]
        if final_norm:
            y = y * lax.rsqrt(jnp.mean(y * y, axis=-1, keepdims=True) + EPS) * fw_ref[...]
        o_ref[...] = y


def _peer_b(hn, x, facs, u_bf, v_bf, final_w, t, rows_per_chunk, final_norm):
    n = hn.shape[0]
    rows = PEER_HEADS * PEER_NKEYS
    ec = rows_per_chunk * PEER_NKEYS
    fac_spec = pl.BlockSpec((rows, t), lambda ti, c: (0, ti))
    tok_spec = pl.BlockSpec((t, D_MODEL), lambda ti, c: (ti, 0))
    exp_spec = pl.BlockSpec((ec, D_MODEL), lambda ti, c: (c, 0))
    return pl.pallas_call(
        functools.partial(_peer_b_kernel, rows_per_chunk=rows_per_chunk, final_norm=final_norm),
        grid=(n // t, PEER_EXPERTS // ec),
        in_specs=[tok_spec, tok_spec, fac_spec, fac_spec, fac_spec, fac_spec, exp_spec, exp_spec,
                  pl.BlockSpec((1, D_MODEL), lambda ti, c: (0, 0))],
        out_specs=tok_spec,
        out_shape=jax.ShapeDtypeStruct((n, D_MODEL), F32),
        scratch_shapes=[pltpu.VMEM((t, D_MODEL), F32), pltpu.VMEM((ec, t), F32)],
        compiler_params=_cparams("arbitrary", "arbitrary"),
        name="peer_experts",
    )(hn, x, *facs, u_bf, v_bf, final_w)


def peer_block(hn, x, wqt, keys2, u_bf, v_bf, final_w, final_norm):
    n = hn.shape[0]
    facs = _peer_a(hn, wqt, keys2, min(PEER_TOPK_TOKENS, n))
    return _peer_b(hn, x, facs, u_bf, v_bf, final_w, min(PEER_EXPERT_TOKENS, n), PEER_ROWS_PER_CHUNK, final_norm)


ROW_CHUNK = 256
Z_COL = {'dz': 3, 'pu': 4, 'rq': 5, 'rk': 6, 'rv': 7, 'rg': 8, 'ca': 9, 'cg': 10}


def _zspec(name, l):
    return pl.BlockSpec((1, l, GROUP_WIDTH), lambda b, c=Z_COL[name]: (b, 0, c))


def _full(shape):
    return pl.BlockSpec(shape, lambda b: (0,) * len(shape))


def _split3(x):
    hi = x.astype(BF16)
    r = x - hi.astype(F32)
    mid = r.astype(BF16)
    lo = (r - mid.astype(F32)).astype(BF16)
    return hi, mid, lo


def _dot3(x, m_bf):
    hi, mid, lo = _split3(x)
    return _dot(hi, m_bf, 1, 0) + _dot(mid, m_bf, 1, 0) + _dot(lo, m_bf, 1, 0)


def _conf_kernel(a_ref, gt_ref, hist_ref, dw_ref, db_ref, lnw_ref, lnb_ref, pw_ref, o_ref, nh_ref, buf):
    l = a_ref.shape[1]
    off = 32
    buf[0:off - (CONF_CONV - 1), :] = jnp.zeros((off - (CONF_CONV - 1), GROUP_WIDTH), F32)
    buf[off - (CONF_CONV - 1):off, :] = hist_ref[0]
    buf[off:off + l, :] = a_ref[0] * jax.nn.sigmoid(gt_ref[0])
    for r in range(l // ROW_CHUNK):
        base = r * ROW_CHUNK + off - (CONF_CONV - 1)
        acc = jnp.zeros((ROW_CHUNK, GROUP_WIDTH), F32) + db_ref[...]
        for w in range(CONF_CONV):
            acc = acc + buf[base + w:base + w + ROW_CHUNK, :] * dw_ref[w:w + 1, :]
        mu = jnp.mean(acc, axis=-1, keepdims=True)
        d = acc - mu
        var = jnp.mean(d * d, axis=-1, keepdims=True)
        hn = d * lax.rsqrt(var + EPS) * lnw_ref[...] + lnb_ref[...]
        act = hn * jax.nn.sigmoid(hn)
        o_ref[0, r * ROW_CHUNK:(r + 1) * ROW_CHUNK, :] = _dot(act.astype(BF16), pw_ref[...], 1, 0)
    nh_ref[0] = buf[l + off - (CONF_CONV - 1):l + off, :]


def _conf_pallas(z, hist, dw, db, lnw, lnb, pw_bf):
    b, l, _ = z.shape
    c = GROUP_WIDTH
    row = lambda v: v.reshape(1, c)
    return pl.pallas_call(
        _conf_kernel,
        grid=(b,),
        in_specs=[_zspec('ca', l), _zspec('cg', l), pl.BlockSpec((1, CONF_CONV - 1, c), lambda i: (i, 0, 0)),
                  _full((CONF_CONV, c)), _full((1, c)), _full((1, c)), _full((1, c)), _full((c, c))],
        out_specs=[pl.BlockSpec((1, l, c), lambda i: (i, 0, 0)), pl.BlockSpec((1, CONF_CONV - 1, c), lambda i: (i, 0, 0))],
        out_shape=[jax.ShapeDtypeStruct((b, l, c), F32), jax.ShapeDtypeStruct((b, CONF_CONV - 1, c), F32)],
        scratch_shapes=[pltpu.VMEM((l + 32, c), F32)],
        compiler_params=_cparams("arbitrary"),
        name="conformer_mixer",
    )(z, z, hist, dw, row(db), row(lnw), row(lnb), pw_bf)


def _pool_kernel(u_ref, hist_ref, wbd_ref, scale_ref, o_ref, nh_ref, buf, *, start_pos):
    l = u_ref.shape[1]
    off = 16
    buf[0:off - POOL_BUF, :] = jnp.zeros((off - POOL_BUF, GROUP_WIDTH), F32)
    buf[off - POOL_BUF:off, :] = hist_ref[0]
    buf[off:off + l, :] = u_ref[0]
    lane = lax.broadcasted_iota(jnp.int32, (ROW_CHUNK, GROUP_WIDTH), 1)
    rowi = lax.broadcasted_iota(jnp.int32, (ROW_CHUNK, GROUP_WIDTH), 0)
    for r in range(l // ROW_CHUNK):
        base = r * ROW_CHUNK + off
        avail = (rowi + (r * ROW_CHUNK + start_pos + 1)).astype(F32)
        cur = buf[base:base + ROW_CHUNK, :]
        s = cur
        mean = None
        for back in range(1, max(POOL_WINDOWS)):
            s = s + buf[base - back:base - back + ROW_CHUNK, :]
            if back + 1 in POOL_WINDOWS:
                gi = POOL_WINDOWS.index(back + 1)
                m = s / jnp.minimum(float(back + 1), avail)
                mean = m if mean is None else jnp.where(lane >= gi * POOL_GROUP, m, mean)
        p = mean - cur
        y = _dot(p.astype(BF16), wbd_ref[...], 1, 0) * scale_ref[...]
        o_ref[0, r * ROW_CHUNK:(r + 1) * ROW_CHUNK, :] = y
    nh_ref[0] = buf[l + off - POOL_BUF:l + off, :]


def _pool_pallas(z, hist, wbd_bf, scale, start_pos):
    b, l, _ = z.shape
    c = GROUP_WIDTH
    return pl.pallas_call(
        functools.partial(_pool_kernel, start_pos=start_pos),
        grid=(b,),
        in_specs=[_zspec('pu', l), pl.BlockSpec((1, POOL_BUF, c), lambda i: (i, 0, 0)), _full((c, c)), _full((1, c))],
        out_specs=[pl.BlockSpec((1, l, c), lambda i: (i, 0, 0)), pl.BlockSpec((1, POOL_BUF, c), lambda i: (i, 0, 0))],
        out_shape=[jax.ShapeDtypeStruct((b, l, c), F32), jax.ShapeDtypeStruct((b, POOL_BUF, c), F32)],
        scratch_shapes=[pltpu.VMEM((l + 16, c), F32)],
        compiler_params=_cparams("arbitrary"),
        name="pool_mixer",
    )(z, hist, wbd_bf, scale.reshape(1, c))


def _ret_kernel(q_ref, k_ref, v_ref, g_ref, cos_ref, sin_ref, perm_ref, dmask_ref, eg_ref, kd_ref, dl_ref, bd_ref,
                s0_ref, o_ref, sn_ref, o_scr, s_scr):
    l = q_ref.shape[1]
    hd = HEAD_DIM
    s_scr[...] = s0_ref[0]

    def per_chunk(ci, carry):
        rows = pl.ds(pl.multiple_of(ci * CHUNK, CHUNK), CHUNK)
        cos = cos_ref[rows, :]
        sin = sin_ref[rows, :]
        q = q_ref[0, rows, :]
        k = k_ref[0, rows, :]
        q = q * cos + _dot3(q, perm_ref[...]) * sin
        k = (k * cos + _dot3(k, perm_ref[...]) * sin) * (hd ** -0.5)
        v = v_ref[0, rows, :]
        for h in range(N_HEADS):
            ls = slice(h * hd, (h + 1) * hd)
            qh, kh, vh = q[:, ls], k[:, ls], v[:, ls].astype(BF16)
            s_h = s_scr[h]
            aqk = _dot(qh.astype(BF16), kh.astype(BF16), 1, 1) * dmask_ref[h]
            o = _dot((qh * eg_ref[h]).astype(BF16), s_h.astype(BF16), 1, 0) + _dot(aqk.astype(BF16), vh, 1, 0)
            s_scr[h] = s_h * dl_ref[h] + _dot((kh * kd_ref[h]).astype(BF16), vh, 0, 0)
            o_scr[rows, ls] = o
        return carry

    lax.fori_loop(0, l // CHUNK, per_chunk, 0)
    sn_ref[0] = s_scr[...]
    for r in range(l // ROW_CHUNK):
        rows = slice(r * ROW_CHUNK, (r + 1) * ROW_CHUNK)
        o = o_scr[rows, :]
        d = o - _dot3(o, bd_ref[...])
        var = _dot3(d * d, bd_ref[...])
        g = g_ref[0, rows, :]
        o_ref[0, rows, :] = g * jax.nn.sigmoid(g) * (d * lax.rsqrt(var + EPS))


def _ret_tables(l, start_pos):
    hd, half = HEAD_DIM, HEAD_DIM // 2
    pos = jnp.arange(l, dtype=F32) + start_pos
    inv = 1.0 / (ROPE_BASE ** (jnp.arange(half, dtype=F32) / half))
    ang = pos[:, None] * inv[None, :]
    cos = jnp.tile(jnp.cos(ang), (1, 2 * N_HEADS))
    sin = jnp.tile(jnp.sin(ang), (1, 2 * N_HEADS))
    lane = np.arange(GROUP_WIDTH)
    perm = np.zeros((GROUP_WIDTH, GROUP_WIDTH), np.float32)
    first = (lane % hd) < half
    perm[lane[first] + half, lane[first]] = -1.0
    perm[lane[~first] - half, lane[~first]] = 1.0
    bd = (lane[:, None] // hd == lane[None, :] // hd).astype(np.float32) / hd
    c = min(CHUNK, l)
    log_gamma = jnp.log1p(-jnp.exp2(-5.0 - jnp.arange(N_HEADS, dtype=F32)))
    G = jnp.cumsum(jnp.broadcast_to(log_gamma[:, None], (N_HEADS, c)), axis=-1)
    idx = jnp.arange(c)
    incl = idx[:, None] >= idx[None, :]
    dmask = jnp.exp(jnp.where(incl, G[:, :, None] - G[:, None, :], -jnp.inf))
    ones = jnp.ones((N_HEADS, c, hd), F32)
    eg = jnp.exp(G)[:, :, None] * ones
    kd = jnp.exp(G[:, -1:] - G)[:, :, None] * ones
    dl = jnp.exp(G[:, -1])[:, None, None] * jnp.ones((N_HEADS, hd, hd), F32)
    return cos, sin, jnp.asarray(perm, BF16), dmask, eg, kd, dl, jnp.asarray(bd, BF16)


def _ret_pallas(z, s0, start_pos):
    b, l, _ = z.shape
    c = GROUP_WIDTH
    tabs = _ret_tables(l, start_pos)
    st_spec = pl.BlockSpec((1, N_HEADS, HEAD_DIM, HEAD_DIM), lambda i: (i, 0, 0, 0))
    return pl.pallas_call(
        _ret_kernel,
        grid=(b,),
        in_specs=[_zspec('rq', l), _zspec('rk', l), _zspec('rv', l), _zspec('rg', l)]
                 + [_full(t.shape) for t in tabs] + [st_spec],
        out_specs=[pl.BlockSpec((1, l, c), lambda i: (i, 0, 0)), st_spec],
        out_shape=[jax.ShapeDtypeStruct((b, l, c), F32), jax.ShapeDtypeStruct(s0.shape, F32)],
        scratch_shapes=[pltpu.VMEM((l, c), F32), pltpu.VMEM((N_HEADS, HEAD_DIM, HEAD_DIM), F32)],
        compiler_params=_cparams("arbitrary"),
        name="retention_mixer",
    )(z, z, z, z, *tabs, s0)


def _split2(x):
    hi = x.astype(BF16)
    return hi, (x - hi.astype(F32)).astype(BF16)


def _dotx(a, b):
    ah, al = _split2(a)
    bh, bl = _split2(b)
    return _dot(ah, bh, 1, 0) + (_dot(ah, bl, 1, 0) + _dot(al, bh, 1, 0))


def _bdot(a, b, ca=1, cb=0):
    return _dot(a.astype(BF16), b.astype(BF16), ca, cb)


def _delta_kernel(qkv_ref, dz_ref, zab_ref, hist_ref, s0_ref, cw_ref, alog_ref, dtb_ref, nw_ref, exp_ref, ones_ref,
                  ltri_ref, utri_ref, o_ref, nh_ref, sn_ref, buf, q_scr, k_scr, v_scr, g_scr, b_scr, o_scr, s_scr):
    l = qkv_ref.shape[1]
    hd, gw = HEAD_DIM, GROUP_WIDTH
    off = 8
    nhist = DELTA_CONV - 1
    buf[0:off - nhist, :] = jnp.zeros((off - nhist, 3 * gw), F32)
    buf[off - nhist:off, :] = hist_ref[0]
    buf[off:off + l, :] = qkv_ref[0]
    for r in range(l // ROW_CHUNK):
        rows = slice(r * ROW_CHUNK, (r + 1) * ROW_CHUNK)
        base = r * ROW_CHUNK + off - nhist
        acc = jnp.zeros((ROW_CHUNK, 3 * gw), F32)
        for w in range(DELTA_CONV):
            acc = acc + buf[base + w:base + w + ROW_CHUNK, :] * cw_ref[w:w + 1, :]
        act = acc * jax.nn.sigmoid(acc)
        q, k = act[:, 0:gw], act[:, gw:2 * gw]
        q_scr[rows, :] = q * lax.rsqrt(_dot3(q * q, ones_ref[...]) + EPS) * (hd ** -0.5)
        k_scr[rows, :] = k * lax.rsqrt(_dot3(k * k, ones_ref[...]) + EPS)
        v_scr[rows, :] = act[:, 2 * gw:3 * gw]
        zab = zab_ref[0, rows, :]
        x = zab + dtb_ref[...]
        softplus = jnp.maximum(x, 0.0) + jnp.log(1.0 + jnp.exp(-jnp.abs(x)))
        g_scr[rows, :] = _dot3(-jnp.exp(alog_ref[...]) * softplus, exp_ref[0])
        b_scr[rows, :] = _dot3(jax.nn.sigmoid(zab), exp_ref[1])
    nh_ref[0] = buf[l + off - nhist:l + off, :]

    s_scr[...] = s0_ref[0]
    ri = lax.broadcasted_iota(jnp.int32, (CHUNK, CHUNK), 0)
    ci_ = lax.broadcasted_iota(jnp.int32, (CHUNK, CHUNK), 1)
    incl = ri >= ci_
    strict = ri > ci_
    eye = jnp.where(ri == ci_, 1.0, 0.0).astype(F32)
    ltri = ltri_ref[...]
    utri = utri_ref[...]

    def per_chunk(ci, carry):
        rows = pl.ds(pl.multiple_of(ci * CHUNK, CHUNK), CHUNK)
        q, k, v = q_scr[rows, :], k_scr[rows, :], v_scr[rows, :]
        gx, bx = g_scr[rows, :], b_scr[rows, :]
        for h in range(N_HEADS):
            ls = slice(h * hd, (h + 1) * hd)
            qh, kh, vh, gh, bh = q[:, ls], k[:, ls], v[:, ls], gx[:, ls], bx[:, ls]
            g3 = _split3(gh)
            gcol = sum(_dot(ltri, p, 1, 0) for p in g3)
            grow = sum(_dot(p, utri, 0, 0) for p in g3)
            dmask = jnp.where(incl, jnp.exp(gcol - grow), 0.0)
            kb = kh * bh
            a = jnp.where(strict, _bdot(kb, kh, 1, 1) * dmask, 0.0)
            t = eye - a
            pw = a
            for _ in range(5):
                pw = _dotx(pw, pw)
                t = t + _dotx(t, pw)
            eg = jnp.exp(gcol)
            u = _bdot(t, vh * bh)
            wm = _bdot(t, kb * eg)
            aqk = _bdot(qh, kh, 1, 1) * dmask
            glast = gcol[CHUNK - 1:CHUNK, :]
            kd = kh * jnp.exp(glast - gcol)
            s_h = s_scr[h]
            v_new = u - _bdot(wm, s_h)
            o_scr[rows, ls] = _bdot(qh * eg, s_h) + _bdot(aqk, v_new)
            s_scr[h] = s_h * jnp.exp(glast) + _bdot(kd, v_new, 0, 0)
        return carry

    lax.fori_loop(0, l // CHUNK, per_chunk, 0)
    sn_ref[0] = s_scr[...]
    for r in range(l // ROW_CHUNK):
        rows = slice(r * ROW_CHUNK, (r + 1) * ROW_CHUNK)
        o = o_scr[rows, :]
        ms = _dot3(o * o, ones_ref[...]) * (1.0 / hd)
        dz = dz_ref[0, rows, :]
        o_ref[0, rows, :] = o * lax.rsqrt(ms + EPS) * nw_ref[...] * (dz * jax.nn.sigmoid(dz))


def _delta_pallas(z, hist, s0, conv_w, a_log, dt_bias, norm_w):
    b, l, _ = z.shape
    c, hd = GROUP_WIDTH, HEAD_DIM
    lane = np.arange(c)
    ones_bd = jnp.asarray(lane[:, None] // hd == lane[None, :] // hd, BF16)
    expand = np.zeros((2, 128, c), np.float32)
    for h in range(N_HEADS):
        expand[0, h, h * hd:(h + 1) * hd] = 1.0
        expand[1, N_HEADS + h, h * hd:(h + 1) * hd] = 1.0
    ltri = jnp.asarray(np.tril(np.ones((CHUNK, CHUNK), np.float32)), BF16)
    pad = lambda v: jnp.zeros((1, 128), F32).at[0, :N_HEADS].set(v)
    st_spec = pl.BlockSpec((1, N_HEADS, hd, hd), lambda i: (i, 0, 0, 0))
    hist_spec = pl.BlockSpec((1, DELTA_CONV - 1, 3 * c), lambda i: (i, 0, 0))
    return pl.pallas_call(
        _delta_kernel,
        grid=(b,),
        in_specs=[pl.BlockSpec((1, l, 3 * c), lambda i: (i, 0, 0)), _zspec('dz', l),
                  pl.BlockSpec((1, l, 128), lambda i: (i, 0, Z_AB // 128)), hist_spec, st_spec,
                  _full((DELTA_CONV, 3 * c)), _full((1, 128)), _full((1, 128)), _full((1, c)),
                  _full((2, 128, c)), _full((c, c)), _full((CHUNK, CHUNK)), _full((CHUNK, CHUNK))],
        out_specs=[pl.BlockSpec((1, l, c), lambda i: (i, 0, 0)), hist_spec, st_spec],
        out_shape=[jax.ShapeDtypeStruct((b, l, c), F32), jax.ShapeDtypeStruct(hist.shape, F32),
                   jax.ShapeDtypeStruct(s0.shape, F32)],
        scratch_shapes=[pltpu.VMEM((l + 8, 3 * c), F32)] + [pltpu.VMEM((l, c), F32)] * 6
                       + [pltpu.VMEM((N_HEADS, hd, hd), F32)],
        compiler_params=_cparams("arbitrary"),
        name="delta_mixer",
    )(z, z, z, hist, s0, conv_w, pad(a_log), pad(dt_bias), jnp.tile(norm_w, N_HEADS).reshape(1, c),
      jnp.asarray(expand, BF16), ones_bd, ltri, ltri.T)


DEC_TOKENS = 64
STATE_FLAT = HEAD_DIM * HEAD_DIM


def _state_tables():
    hd = HEAD_DIM
    col = np.arange(STATE_FLAT)
    rep = (col[None, :] // hd == np.arange(hd)[:, None]).astype(np.float32)
    tile = (col[None, :] % hd == np.arange(hd)[:, None]).astype(np.float32)
    return jnp.asarray(rep, BF16), jnp.asarray(tile, BF16), jnp.asarray(tile.T, BF16)


def _state_step(s_ref, so_ref, h, w_read, q_read, k_write, decay, u, aqk, rep, tile, red):
    s = s_ref[:, h, :]
    v_new = u if w_read is None else u - _dot3(_dot3(w_read, rep) * s, red)
    o = _dot3(_dot3(q_read, rep) * s, red) + aqk * v_new
    so_ref[:, h, :] = s * _dot3(decay, rep) + _dot3(k_write, rep) * _dot3(v_new, tile)
    return o


def _dec_delta_kernel(z_ref, hist_ref, s_ref, cw_ref, alog_ref, dtb_ref, nw_ref, exp_ref, ones_ref, rep_ref, tile_ref,
                      red_ref, o_ref, nh_ref, so_ref):
    hd, gw = HEAD_DIM, GROUP_WIDTH
    qkv = z_ref[:, 0:3 * gw]
    acc = qkv * cw_ref[DELTA_CONV - 1:DELTA_CONV, :]
    for w in range(DELTA_CONV - 1):
        acc = acc + hist_ref[:, w * 3 * gw:(w + 1) * 3 * gw] * cw_ref[w:w + 1, :]
    nh_ref[:, 0:2 * 3 * gw] = hist_ref[:, 3 * gw:3 * 3 * gw]
    nh_ref[:, 2 * 3 * gw:3 * 3 * gw] = qkv
    act = acc * jax.nn.sigmoid(acc)
    q, k, v = act[:, 0:gw], act[:, gw:2 * gw], act[:, 2 * gw:3 * gw]
    q = q * lax.rsqrt(_dot3(q * q, ones_ref[...]) + EPS) * (hd ** -0.5)
    k = k * lax.rsqrt(_dot3(k * k, ones_ref[...]) + EPS)
    zab = z_ref[:, Z_AB:Z_AB + 128]
    x = zab + dtb_ref[...]
    softplus = jnp.maximum(x, 0.0) + jnp.log(1.0 + jnp.exp(-jnp.abs(x)))
    eg = jnp.exp(_dot3(-jnp.exp(alog_ref[...]) * softplus, exp_ref[0]))
    beta = _dot3(jax.nn.sigmoid(zab), exp_ref[1])
    aqk = _dot3(q * k, ones_ref[...])
    kb = k * beta
    outs = []
    for h in range(N_HEADS):
        ls = slice(h * hd, (h + 1) * hd)
        outs.append(_state_step(s_ref, so_ref, h, (kb * eg)[:, ls], (q * eg)[:, ls], k[:, ls], eg[:, ls],
                                (v * beta)[:, ls], aqk[:, ls], rep_ref[...], tile_ref[...], red_ref[...]))
    o = jnp.concatenate(outs, axis=1)
    ms = _dot3(o * o, ones_ref[...]) * (1.0 / hd)
    dz = z_ref[:, 3 * gw:4 * gw]
    o_ref[...] = o * lax.rsqrt(ms + EPS) * nw_ref[...] * (dz * jax.nn.sigmoid(dz))


def _dec_ret_kernel(z_ref, s_ref, cos_ref, sin_ref, perm_ref, eg_ref, ones_ref, rep_ref, tile_ref, red_ref,
                    o_ref, so_ref):
    hd, gw = HEAD_DIM, GROUP_WIDTH
    q = z_ref[:, 5 * gw:6 * gw]
    k = z_ref[:, 6 * gw:7 * gw]
    v = z_ref[:, 7 * gw:8 * gw]
    g = z_ref[:, 8 * gw:9 * gw]
    q = q * cos_ref[...] + _dot3(q, perm_ref[...]) * sin_ref[...]
    k = (k * cos_ref[...] + _dot3(k, perm_ref[...]) * sin_ref[...]) * (hd ** -0.5)
    aqk = _dot3(q * k, ones_ref[...])
    eg = jnp.broadcast_to(eg_ref[...], q.shape)
    outs = []
    for h in range(N_HEADS):
        ls = slice(h * hd, (h + 1) * hd)
        outs.append(_state_step(s_ref, so_ref, h, None, (q * eg)[:, ls], k[:, ls], eg[:, ls], v[:, ls], aqk[:, ls],
                                rep_ref[...], tile_ref[...], red_ref[...]))
    o = jnp.concatenate(outs, axis=1)
    d = o - _dot3(o, ones_ref[...]) * (1.0 / hd)
    var = _dot3(d * d, ones_ref[...]) * (1.0 / hd)
    o_ref[...] = g * jax.nn.sigmoid(g) * (d * lax.rsqrt(var + EPS))


def _dec_pool_conf_kernel(z_ref, ph_ref, ch_ref, wbd_ref, scale_ref, dw_ref, db_ref, lnw_ref, lnb_ref, pw_ref,
                          po_ref, pnh_ref, co_ref, cnh_ref, *, start_pos):
    gw = GROUP_WIDTH
    u = z_ref[:, 4 * gw:5 * gw]
    lane = lax.broadcasted_iota(jnp.int32, u.shape, 1)
    s = u
    mean = None
    for back in range(1, max(POOL_WINDOWS)):
        s = s + ph_ref[:, (POOL_BUF - back) * gw:(POOL_BUF - back + 1) * gw]
        if back + 1 in POOL_WINDOWS:
            gi = POOL_WINDOWS.index(back + 1)
            m = s / float(min(back + 1, start_pos + 1))
            mean = m if mean is None else jnp.where(lane >= gi * POOL_GROUP, m, mean)
    po_ref[...] = _dot((mean - u).astype(BF16), wbd_ref[...], 1, 0) * scale_ref[...]
    pnh_ref[:, 0:(POOL_BUF - 1) * gw] = ph_ref[:, gw:POOL_BUF * gw]
    pnh_ref[:, (POOL_BUF - 1) * gw:POOL_BUF * gw] = u

    nh = CONF_CONV - 1
    glu = z_ref[:, 9 * gw:10 * gw] * jax.nn.sigmoid(z_ref[:, 10 * gw:11 * gw])
    acc = glu * dw_ref[nh:nh + 1, :] + db_ref[...]
    for w in range(nh):
        acc = acc + ch_ref[:, w * gw:(w + 1) * gw] * dw_ref[w:w + 1, :]
    mu = jnp.mean(acc, axis=-1, keepdims=True)
    d = acc - mu
    var = jnp.mean(d * d, axis=-1, keepdims=True)
    hn = d * lax.rsqrt(var + EPS) * lnw_ref[...] + lnb_ref[...]
    co_ref[...] = _dot((hn * jax.nn.sigmoid(hn)).astype(BF16), pw_ref[...], 1, 0)
    cnh_ref[:, 0:(nh - 1) * gw] = ch_ref[:, gw:nh * gw]
    cnh_ref[:, (nh - 1) * gw:nh * gw] = glu


def _decode_mixers(z, states, p, start_pos):
    n = z.shape[0]
    s_delta, s_dconv, s_pool, s_ret, s_conv = states
    c, hd, bt = GROUP_WIDTH, HEAD_DIM, DEC_TOKENS
    tok = lambda width: pl.BlockSpec((bt, width), lambda i: (i, 0))
    st_spec = pl.BlockSpec((bt, N_HEADS, STATE_FLAT), lambda i: (i, 0, 0))
    st_shape = jax.ShapeDtypeStruct((n, N_HEADS, STATE_FLAT), F32)
    lane = np.arange(c)
    ones_bd = jnp.asarray(lane[:, None] // hd == lane[None, :] // hd, BF16)
    expand = np.zeros((2, 128, c), np.float32)
    for h in range(N_HEADS):
        expand[0, h, h * hd:(h + 1) * hd] = 1.0
        expand[1, N_HEADS + h, h * hd:(h + 1) * hd] = 1.0
    pad = lambda v: jnp.zeros((1, 128), F32).at[0, :N_HEADS].set(v)
    row = lambda v: v.reshape(1, c)
    rep, tile, red = _state_tables()
    tab_specs = [_full(rep.shape), _full(tile.shape), _full(red.shape)]

    o_a, n_dconv, n_delta = pl.pallas_call(
        _dec_delta_kernel,
        grid=(n // bt,),
        in_specs=[tok(Z_WIDTH), tok(3 * 3 * c), st_spec, _full((DELTA_CONV, 3 * c)), _full((1, 128)), _full((1, 128)),
                  _full((1, c)), _full((2, 128, c)), _full((c, c))] + tab_specs,
        out_specs=[tok(c), tok(3 * 3 * c), st_spec],
        out_shape=[jax.ShapeDtypeStruct((n, c), F32), jax.ShapeDtypeStruct((n, 3 * 3 * c), F32), st_shape],
        compiler_params=_cparams("arbitrary"),
        name="decode_delta",
    )(z, s_dconv.reshape(n, -1), s_delta.reshape(n, N_HEADS, STATE_FLAT), p['delta_conv_w'], pad(p['delta_a_log']),
      pad(p['delta_dt_bias']), row(jnp.tile(p['delta_norm_w'], N_HEADS)), jnp.asarray(expand, BF16), ones_bd,
      rep, tile, red)

    cos, sin, perm, _, _, _, _, _ = _ret_tables(1, start_pos)
    log_gamma = jnp.log1p(-jnp.exp2(-5.0 - jnp.arange(N_HEADS, dtype=F32)))
    eg_row = jnp.repeat(jnp.exp(log_gamma), hd).reshape(1, c)
    o_c, n_ret = pl.pallas_call(
        _dec_ret_kernel,
        grid=(n // bt,),
        in_specs=[tok(Z_WIDTH), st_spec, _full((1, c)), _full((1, c)), _full((c, c)), _full((1, c)), _full((c, c))]
                 + tab_specs,
        out_specs=[tok(c), st_spec],
        out_shape=[jax.ShapeDtypeStruct((n, c), F32), st_shape],
        compiler_params=_cparams("arbitrary"),
        name="decode_retention",
    )(z, s_ret.reshape(n, N_HEADS, STATE_FLAT), cos, sin, perm, eg_row, ones_bd, rep, tile, red)

    o_b, n_pool, o_d, n_conv = pl.pallas_call(
        functools.partial(_dec_pool_conf_kernel, start_pos=start_pos),
        grid=(n // bt,),
        in_specs=[tok(Z_WIDTH), tok(POOL_BUF * c), tok((CONF_CONV - 1) * c), _full((c, c)), _full((1, c)),
                  _full((CONF_CONV, c)), _full((1, c)), _full((1, c)), _full((1, c)), _full((c, c))],
        out_specs=[tok(c), tok(POOL_BUF * c), tok(c), tok((CONF_CONV - 1) * c)],
        out_shape=[jax.ShapeDtypeStruct((n, c), F32), jax.ShapeDtypeStruct((n, POOL_BUF * c), F32),
                   jax.ShapeDtypeStruct((n, c), F32), jax.ShapeDtypeStruct((n, (CONF_CONV - 1) * c), F32)],
        compiler_params=_cparams("arbitrary"),
        name="decode_pool_conformer",
    )(z, s_pool.reshape(n, -1), s_conv.reshape(n, -1), p['pool_wbd_bf'], row(p['pool_scale']), p['conv_dw_w'],
      row(p['conv_dw_b']), row(p['conv_ln_w']), row(p['conv_ln_b']), p['conv_pw_bf'])

    mix = jnp.concatenate([o_a, o_b, o_c, o_d], axis=-1)
    new = (n_delta.reshape(s_delta.shape), n_dconv.reshape(s_dconv.shape), n_pool.reshape(s_pool.shape),
           n_ret.reshape(s_ret.shape), n_conv.reshape(s_conv.shape))
    return mix, new


def _pool_blockdiag(pool_w):
    out = jnp.zeros((GROUP_WIDTH, GROUP_WIDTH), pool_w.dtype)
    for gi in range(len(POOL_WINDOWS)):
        sl = slice(gi * POOL_GROUP, (gi + 1) * POOL_GROUP)
        out = out.at[sl, sl].set(pool_w[gi])
    return out


def _permute_w_in(w_in):
    pad = jnp.zeros((w_in.shape[0], Z_WIDTH - Z_AB - 8), w_in.dtype)
    return jnp.concatenate([w_in[:, :OFF_DA], w_in[:, OFF_PU:], w_in[:, OFF_DA:OFF_PU], pad], axis=1)


def _run_trunk(x, states, params, final_norm, start_pos):
    b, l, _ = x.shape
    n = b * l
    xf = x.reshape(n, D_MODEL)
    s_delta, s_dconv, s_pool, s_ret, s_conv = states
    new = ([], [], [], [], [])
    for li in range(DEPTH):
        p = {name: arr[li] for name, arr in params.items()}
        z = _in_proj(xf, p['norm1'].reshape(1, D_MODEL), p['w_in_bf'])
        if l == 1:
            mix, (n_delta, n_dconv, n_pool, n_ret, n_conv) = _decode_mixers(
                z, (s_delta[li], s_dconv[li], s_pool[li], s_ret[li], s_conv[li]), p, start_pos)
        else:
            assert l % ROW_CHUNK == 0, "sequence kernels process whole ROW_CHUNK blocks"
            z = z.reshape(b, l, Z_WIDTH)
            o_a, n_dconv, n_delta = _delta_pallas(z, s_dconv[li], s_delta[li], p['delta_conv_w'], p['delta_a_log'],
                                                  p['delta_dt_bias'], p['delta_norm_w'])
            o_b, n_pool = _pool_pallas(z, s_pool[li], p['pool_wbd_bf'], p['pool_scale'], start_pos)
            o_c, n_ret = _ret_pallas(z, s_ret[li], start_pos)
            o_d, n_conv = _conf_pallas(z, s_conv[li], p['conv_dw_w'], p['conv_dw_b'], p['conv_ln_w'],
                                       p['conv_ln_b'], p['conv_pw_bf'])
            mix = jnp.concatenate([o_a, o_b, o_c, o_d], axis=-1).reshape(n, D_MODEL)
        xf, hn = _out_proj(xf, mix, p['w_out_bf'], p['norm2'].reshape(1, D_MODEL))
        xf = peer_block(hn, xf, p['wqt_bf'], p['keys_bf'], p['u_bf'], p['v_bf'], final_norm.reshape(1, D_MODEL),
                        li == DEPTH - 1)
        for acc, a in zip(new, (n_delta, n_dconv, n_pool, n_ret, n_conv)):
            acc.append(a)
    return xf.reshape(b, l, D_MODEL), tuple(jnp.stack(acc) for acc in new)


def kernel(x_prompt, x_sample, state_delta, state_delta_conv, state_pool, state_ret, state_conv, norm1, w_in, delta_conv_w, delta_a_log, delta_dt_bias, delta_norm_w, pool_w, pool_scale, conv_dw_w, conv_dw_b, conv_ln_w, conv_ln_b, conv_pw_w, w_out, norm2, peer_wq, peer_keys, peer_u, peer_v, final_norm):
    params = {
        'norm1': norm1, 'delta_conv_w': delta_conv_w, 'delta_a_log': delta_a_log, 'delta_dt_bias': delta_dt_bias,
        'delta_norm_w': delta_norm_w, 'pool_scale': pool_scale, 'conv_dw_w': conv_dw_w,
        'conv_dw_b': conv_dw_b, 'conv_ln_w': conv_ln_w, 'conv_ln_b': conv_ln_b,
        'norm2': norm2,
        'w_in_bf': jax.vmap(_permute_w_in)(w_in).astype(BF16),
        'w_out_bf': w_out.astype(BF16),
        'pool_wbd_bf': jax.vmap(_pool_blockdiag)(pool_w).astype(BF16),
        'conv_pw_bf': conv_pw_w.astype(BF16),
        'wqt_bf': jnp.swapaxes(peer_wq, 1, 2).astype(BF16),
        'keys_bf': peer_keys.reshape(DEPTH, 2 * PEER_HEADS * PEER_NKEYS, PEER_DHALF).astype(BF16),
        'u_bf': peer_u.astype(BF16),
        'v_bf': peer_v.astype(BF16),
    }
    bp = x_prompt.shape[0]
    zeros = tuple(jnp.zeros((DEPTH, bp) + s.shape[2:], F32)
                  for s in (state_delta, state_delta_conv, state_pool, state_ret, state_conv))
    y_p, st_p = _run_trunk(x_prompt, zeros, params, final_norm, 0)
    y_s, st_s = _run_trunk(x_sample, (state_delta, state_delta_conv, state_pool, state_ret, state_conv),
                           params, final_norm, PAST_LEN)
    return (y_p, y_s) + st_p + st_s
```
